```python
import math
import jax, jax.numpy as jnp
from jax import lax
import numpy as np

D_MODEL = 1024
BATCH = 8
SEQ = 4096
DEPTH = 1

CHUNK = 64
EPS = 1e-6

DA_HEADS = 8
DA_DQK = 64
DA_DV = 2 * DA_DQK
DA_WIDTH = DA_HEADS * DA_DV
Q_BLOCK = 128

SG_BLOCK = 128
SG_GROUPS = 8
SG_WIDTH = D_MODEL
SG_GROUP_DIM = SG_WIDTH // SG_GROUPS

SPLIT_SIZES = (2 * DA_HEADS * DA_DQK,
               2 * DA_HEADS * DA_DQK,
               DA_WIDTH,
               SG_WIDTH,
               SG_WIDTH,
               2 * D_MODEL)
D_IN = sum(SPLIT_SIZES)

PEER_HEADS = 8
N_KEYS = 128
N_EXPERTS = N_KEYS * N_KEYS
PEER_TOPK = 16
D_KEY = 256
D_KEY_HALF = D_KEY // 2
PEER_TOKEN_BLOCK = 128

kernel_name = "hybrid_diffattn_gmlp_peer_block"


def rmsnorm(x, g):
    xf = x.astype(jnp.float32)
    y = xf * lax.rsqrt(jnp.mean(xf * xf, axis=-1, keepdims=True) + EPS)
    return (y * g).astype(x.dtype)


def layernorm(x, g, b):
    xf = x.astype(jnp.float32)
    mu = jnp.mean(xf, axis=-1, keepdims=True)
    var = jnp.mean(jnp.square(xf - mu), axis=-1, keepdims=True)
    y = (xf - mu) * lax.rsqrt(var + EPS)
    return (y * g + b).astype(x.dtype)


def lambda_init_for(layer):
    return 0.8 - 0.6 * math.exp(-0.3 * layer)


def diff_attention(q, k, v, lam, subln_g, lambda_init):
    B, S = q.shape[0], q.shape[1]
    nb = S // Q_BLOCK
    scale = DA_DQK ** -0.5
    qb = q.reshape(B, nb, Q_BLOCK, 2, DA_HEADS, DA_DQK).transpose(1, 0, 2, 3, 4, 5)
    key_chunk = jnp.arange(S) // CHUNK

    def block(args):
        qblk, bi = args
        q_chunk = (bi * Q_BLOCK + jnp.arange(Q_BLOCK)) // CHUNK
        allowed = key_chunk[None, :] <= q_chunk[:, None]
        s = jnp.einsum('bqmhd,bkmhd->bmhqk', qblk, k).astype(jnp.float32) * scale
        s = jnp.where(allowed, s, -1e30)
        p = jax.nn.softmax(s, axis=-1)
        a = p[:, 0] - lam * p[:, 1]
        return jnp.einsum('bhqk,bkhd->bqhd', a.astype(v.dtype), v)

    o = lax.map(block, (qb, jnp.arange(nb)))
    o = o.transpose(1, 0, 2, 3, 4).reshape(B, S, DA_HEADS, DA_DV)
    o = rmsnorm(o, subln_g) * (1.0 - lambda_init)
    return o.reshape(B, S, DA_WIDTH)


def spatial_gating(u, v, ln_g, ln_b, w_s, b_s):
    v = layernorm(v, ln_g, ln_b)
    B, S = v.shape[0], v.shape[1]
    nb = S // SG_BLOCK
    vb = v.reshape(B, nb, SG_BLOCK, SG_GROUPS, SG_GROUP_DIM)
    pos_chunk = jnp.arange(SG_BLOCK) // CHUNK
    mask = pos_chunk[:, None] >= pos_chunk[None, :]
    w = jnp.where(mask[None], w_s, 0)
    mixed = jnp.einsum('gij,bnjgc->bnigc', w, vb) + b_s.T[None, None, :, :, None]
    return u * mixed.reshape(B, S, SG_WIDTH)


def peer(x, w_query, subkeys, u_tab, v_tab):
    B, S, D = x.shape
    T = B * S
    xt = x.reshape(T, D)
    q = (xt @ w_query).reshape(T, PEER_HEADS, 2, D_KEY_HALF)
    s = jnp.einsum('thpc,pnc->thpn', q, subkeys).astype(jnp.float32)
    sv, si = lax.top_k(s, PEER_TOPK)
    cand = sv[:, :, 0, :, None] + sv[:, :, 1, None, :]
    cand_idx = si[:, :, 0, :, None] * N_KEYS + si[:, :, 1, None, :]
    cand = cand.reshape(T, PEER_HEADS, PEER_TOPK * PEER_TOPK)
    cand_idx = cand_idx.reshape(T, PEER_HEADS, PEER_TOPK * PEER_TOPK)
    top_s, pos = lax.top_k(cand, PEER_TOPK)
    experts = jnp.take_along_axis(cand_idx, pos, axis=-1)
    g = jax.nn.softmax(top_s, axis=-1)
    nb = T // PEER_TOKEN_BLOCK

    def block(args):
        xb, eb, gb = args
        u = u_tab[eb]
        hdn = jax.nn.gelu(jnp.einsum('td,thkd->thk', xb, u), approximate=False)
        wgt = gb.astype(xb.dtype) * hdn
        return jnp.einsum('thk,thkd->td', wgt, v_tab[eb])

    y = lax.map(block, (xt.reshape(nb, PEER_TOKEN_BLOCK, D),
                        experts.reshape(nb, PEER_TOKEN_BLOCK, PEER_HEADS, PEER_TOPK),
                        g.reshape(nb, PEER_TOKEN_BLOCK, PEER_HEADS, PEER_TOPK)))
    return y.reshape(B, S, D)


def setup_inputs(seed: int = 0) -> dict:
    key = jax.random.key(seed)
    ks = jax.random.split(key, 24)
    f32 = jnp.float32
    L = DEPTH

    def nrm(k, shape, scale):
        return jax.random.normal(k, shape, f32) * scale

    return {
        "x": jax.random.normal(ks[0], (BATCH, SEQ, D_MODEL), f32),
        "norm1_g": 1.0 + nrm(ks[1], (L, D_MODEL), 0.02),
        "w_in": nrm(ks[2], (L, D_MODEL, D_IN), D_MODEL ** -0.5),
        "lambda_q1": nrm(ks[3], (L, DA_DQK), 0.1),
        "lambda_k1": nrm(ks[4], (L, DA_DQK), 0.1),
        "lambda_q2": nrm(ks[5], (L, DA_DQK), 0.1),
        "lambda_k2": nrm(ks[6], (L, DA_DQK), 0.1),
        "da_subln_g": 1.0 + nrm(ks[7], (L, DA_DV), 0.02),
        "sg_ln_g": 1.0 + nrm(ks[8], (L, SG_WIDTH), 0.02),
        "sg_ln_b": nrm(ks[9], (L, SG_WIDTH), 0.02),
        "sg_w": nrm(ks[10], (L, SG_GROUPS, SG_BLOCK, SG_BLOCK), 0.5 * SG_BLOCK ** -0.5),
        "sg_b": 1.0 + nrm(ks[11], (L, SG_GROUPS, SG_BLOCK), 0.02),
        "w_branch_attn": nrm(ks[12], (L, DA_WIDTH, D_MODEL), DA_WIDTH ** -0.5),
        "w_branch_sg": nrm(ks[13], (L, SG_WIDTH, D_MODEL), SG_WIDTH ** -0.5),
        "w_out": nrm(ks[14], (L, D_MODEL, D_MODEL), D_MODEL ** -0.5),
        "norm2_g": 1.0 + nrm(ks[15], (L, D_MODEL), 0.02),
        "peer_w_query": nrm(ks[16], (L, D_MODEL, PEER_HEADS * D_KEY), D_MODEL ** -0.5),
        "peer_subkeys": nrm(ks[17], (L, 2, N_KEYS, D_KEY_HALF), D_KEY_HALF ** -0.5),
        "peer_u": nrm(ks[18], (L, N_EXPERTS, D_MODEL), D_MODEL ** -0.5),
        "peer_v": nrm(ks[19], (L, N_EXPERTS, D_MODEL), 0.3),
        "final_g": 1.0 + nrm(ks[20], (D_MODEL,), 0.02),
    }


def reference(x, norm1_g, w_in, lambda_q1, lambda_k1, lambda_q2, lambda_k2, da_subln_g,
              sg_ln_g, sg_ln_b, sg_w, sg_b, w_branch_attn, w_branch_sg, w_out,
              norm2_g, peer_w_query, peer_subkeys, peer_u, peer_v, final_g):
    B, S, D = x.shape
    offsets = []
    acc = 0
    for sz in SPLIT_SIZES[:-1]:
        acc += sz
        offsets.append(acc)
    h = x
    for l in range(DEPTH):
        lam_init = lambda_init_for(l)
        xn = rmsnorm(h, norm1_g[l])
        proj = xn @ w_in[l]
        q, k, v, su, sv, gates = jnp.split(proj, offsets, axis=-1)
        q = q.reshape(B, S, 2, DA_HEADS, DA_DQK)
        k = k.reshape(B, S, 2, DA_HEADS, DA_DQK)
        v = v.reshape(B, S, DA_HEADS, DA_DV)
        lam = (jnp.exp(jnp.sum(lambda_q1[l].astype(jnp.float32) * lambda_k1[l].astype(jnp.float32)))
               - jnp.exp(jnp.sum(lambda_q2[l].astype(jnp.float32) * lambda_k2[l].astype(jnp.float32)))
               + lam_init)
        ya = diff_attention(q, k, v, lam, da_subln_g[l], lam_init)
        yb = spatial_gating(jax.nn.gelu(su, approximate=False), jax.nn.gelu(sv, approximate=False),
                            sg_ln_g[l], sg_ln_b[l], sg_w[l], sg_b[l])
        ga, gb = jnp.split(jax.nn.sigmoid(gates), 2, axis=-1)
        merged = ga * (ya @ w_branch_attn[l]) + gb * (yb @ w_branch_sg[l])
        h = h + merged @ w_out[l]
        h = h + peer(rmsnorm(h, norm2_g[l]), peer_w_query[l], peer_subkeys[l], peer_u[l], peer_v[l])
    return rmsnorm(h, final_g)
```

```python
import functools
import math

import jax
import jax.numpy as jnp
from jax import lax
from jax.experimental import pallas as pl
from jax.experimental.pallas import tpu as pltpu

F32 = jnp.float32
BF16 = jnp.bfloat16
I32 = jnp.int32

D_MODEL = 1024
CHUNK = 64
EPS = 1e-6
DA_HEADS = 8
DA_DQK = 64
DA_DV = 128
SG_BLOCK = 128
SG_GROUPS = 8
PEER_HEADS = 8
N_KEYS = 128
PEER_TOPK = 16
D_KEY_HALF = 128
N_SEL = PEER_HEADS * PEER_TOPK
LANES = 128
WORDS_PER_ROW = D_MODEL // 2
ROW_SUBLANES = WORDS_PER_ROW // LANES
TILE_STRIDE = N_SEL + 8
VMEM_LIMIT = 56 * 1024 * 1024
LAMBDA_INIT = 0.8 - 0.6 * math.exp(-0.3 * 0)
SQRT_HALF = 0.7071067811865476
HI_MASK = -65536


def _gelu(x):
    return 0.5 * x * (1.0 + lax.erf(x * SQRT_HALF))


def _params(*sem):
    return pltpu.CompilerParams(dimension_semantics=sem, vmem_limit_bytes=VMEM_LIMIT)


def _pack_kernel(t_ref, o_ref):
    t = t_ref[...]
    lo = lax.bitcast_convert_type(t[:, :WORDS_PER_ROW].astype(BF16).astype(F32), I32)
    hi = lax.bitcast_convert_type(t[:, WORDS_PER_ROW:].astype(BF16).astype(F32), I32)
    o_ref[...] = lax.shift_right_logical(lo, 16) | (hi & HI_MASK)


def _pack_table(tab, rows=512):
    n = tab.shape[0]
    out = pl.pallas_call(
        _pack_kernel,
        grid=(n // rows,),
        in_specs=[pl.BlockSpec((rows, D_MODEL), lambda i: (i, 0))],
        out_specs=pl.BlockSpec((rows, WORDS_PER_ROW), lambda i: (i, 0)),
        out_shape=jax.ShapeDtypeStruct((n, WORDS_PER_ROW), I32),
        compiler_params=_params("parallel"),
        name="pack_table",
    )(tab)
    return out.reshape(n * ROW_SUBLANES, LANES)


def _inproj_kernel(x_ref, g_ref, w_ref, lng_ref, lnb_ref, o_ref, xn_ref):
    j = pl.program_id(1)

    @pl.when(j == 0)
    def _():
        x = x_ref[...]
        ms = jnp.mean(x * x, axis=-1, keepdims=True)
        xn_ref[...] = (x * lax.rsqrt(ms + EPS) * g_ref[...]).astype(BF16)

    acc = jnp.dot(xn_ref[...], w_ref[...], preferred_element_type=F32)

    @pl.when(j == 0)
    def _():
        o_ref[...] = (acc * (DA_DQK ** -0.5)).astype(BF16)

    @pl.when((j == 1) | (j == 2))
    def _():
        o_ref[...] = acc.astype(BF16)

    @pl.when(j == 3)
    def _():
        o_ref[...] = _gelu(acc).astype(BF16)

    @pl.when(j == 4)
    def _():
        v = _gelu(acc)
        mu = jnp.mean(v, axis=-1, keepdims=True)
        var = jnp.mean(jnp.square(v - mu), axis=-1, keepdims=True)
        o_ref[...] = ((v - mu) * lax.rsqrt(var + EPS) * lng_ref[...] + lnb_ref[...]).astype(BF16)

    @pl.when(j >= 5)
    def _():
        o_ref[...] = (1.0 / (1.0 + jnp.exp(-acc))).astype(BF16)


def _inproj(x2, g1, w, lng, lnb, tm=512):
    t = x2.shape[0]
    ncol = w.shape[1] // D_MODEL
    return pl.pallas_call(
        _inproj_kernel,
        grid=(t // tm, ncol),
        in_specs=[
            pl.BlockSpec((tm, D_MODEL), lambda i, j: (i, 0)),
            pl.BlockSpec((1, D_MODEL), lambda i, j: (0, 0)),
            pl.BlockSpec((D_MODEL, D_MODEL), lambda i, j: (0, j)),
            pl.BlockSpec((1, D_MODEL), lambda i, j: (0, 0)),
            pl.BlockSpec((1, D_MODEL), lambda i, j: (0, 0)),
        ],
        out_specs=pl.BlockSpec((tm, D_MODEL), lambda i, j: (i, j)),
        out_shape=jax.ShapeDtypeStruct((t, w.shape[1]), BF16),
        scratch_shapes=[pltpu.VMEM((tm, D_MODEL), BF16)],
        compiler_params=_params("parallel", "arbitrary"),
        name="inproj",
    )(x2, g1, w, lng, lnb)


def _attn_kernel(q_ref, k_ref, v_ref, lq1_ref, lk1_ref, lq2_ref, lk2_ref, sg_ref, o_ref,
                 acc_ref, m_ref, l_ref, *, tq):
    i = pl.program_id(2)
    q = q_ref[...]
    lane = lax.broadcasted_iota(I32, (tq, LANES), 1)
    zero = jnp.zeros_like(q)
    qbd = jnp.concatenate([jnp.where(lane < DA_DQK, q, zero), jnp.where(lane >= DA_DQK, q, zero)], axis=0)
    m_ref[...] = jnp.full(m_ref.shape, -jnp.inf, F32)
    l_ref[...] = jnp.zeros(l_ref.shape, F32)
    acc_ref[...] = jnp.zeros(acc_ref.shape, F32)

    def step(kstart, mask):
        k = k_ref[pl.ds(kstart, tq), :]
        v = v_ref[pl.ds(kstart, tq), :]
        s = lax.dot_general(qbd, k, (((1,), (1,)), ((), ())), preferred_element_type=F32)
        if mask is not None:
            s = jnp.where(mask, s, -1e30)
        m_prev = m_ref[...]
        m_new = jnp.maximum(m_prev, jnp.max(s, axis=1, keepdims=True))
        alpha = jnp.exp(m_prev - m_new)
        p = jnp.exp(s - m_new)
        l_ref[...] = alpha * l_ref[...] + jnp.sum(p, axis=1, keepdims=True)
        acc_ref[...] = alpha * acc_ref[...] + jnp.dot(p.astype(BF16), v, preferred_element_type=F32)
        m_ref[...] = m_new

    def body(jj, carry):
        step(pl.multiple_of(jj * tq, tq), None)
        return carry

    lax.fori_loop(0, i, body, 0)
    row = lax.broadcasted_iota(I32, (2 * tq, tq), 0)
    col = lax.broadcasted_iota(I32, (2 * tq, tq), 1)
    qpos = jnp.where(row >= tq, row - tq, row)
    step(pl.multiple_of(i * tq, tq), (col // CHUNK) <= (qpos // CHUNK))

    acc = acc_ref[...]
    l = l_ref[...]
    o1 = acc[:tq] / l[:tq]
    o2 = acc[tq:] / l[tq:]
    lam = (jnp.exp(jnp.sum(lq1_ref[...] * lk1_ref[...], axis=-1, keepdims=True))
           - jnp.exp(jnp.sum(lq2_ref[...] * lk2_ref[...], axis=-1, keepdims=True)) + LAMBDA_INIT)
    o = o1 - lam * o2
    ms = jnp.mean(o * o, axis=-1, keepdims=True)
    o_ref[...] = (o * lax.rsqrt(ms + EPS) * sg_ref[...] * (1.0 - LAMBDA_INIT)).astype(BF16)


def _attention(proj3, lq1, lk1, lq2, lk2, subln_g, tq=256):
    b, s, _ = proj3.shape
    lam_spec = pl.BlockSpec((1, DA_DQK), lambda bi, h, i: (0, 0))
    return pl.pallas_call(
        functools.partial(_attn_kernel, tq=tq),
        grid=(b, DA_HEADS, s // tq),
        in_specs=[
            pl.BlockSpec((None, tq, LANES), lambda bi, h, i: (bi, i, h)),
            pl.BlockSpec((None, s, LANES), lambda bi, h, i: (bi, 0, DA_HEADS + h)),
            pl.BlockSpec((None, s, LANES), lambda bi, h, i: (bi, 0, 2 * DA_HEADS + h)),
            lam_spec, lam_spec, lam_spec, lam_spec,
            pl.BlockSpec((1, DA_DV), lambda bi, h, i: (0, 0)),
        ],
        out_specs=pl.BlockSpec((None, tq, DA_DV), lambda bi, h, i: (bi, i, h)),
        out_shape=jax.ShapeDtypeStruct((b, s, DA_HEADS * DA_DV), BF16),
        scratch_shapes=[pltpu.VMEM((2 * tq, DA_DV), F32), pltpu.VMEM((2 * tq, 1), F32),
                        pltpu.VMEM((2 * tq, 1), F32)],
        compiler_params=_params("parallel", "parallel", "arbitrary"),
        name="diff_attn",
    )(proj3, proj3, proj3, lq1, lk1, lq2, lk2, subln_g)


def _mix_kernel(su_ref, sv_ref, ga_ref, gb_ref, ya_ref, x_ref, sgw_ref, sgb_ref, wa_ref, wb_ref,
                wo_ref, g2_ref, wq_ref, h_ref, xn_ref, qp_ref, yb_ref, *, tm):
    r = lax.broadcasted_iota(I32, (SG_BLOCK, SG_BLOCK), 0)
    c = lax.broadcasted_iota(I32, (SG_BLOCK, SG_BLOCK), 1)
    causal = (r // CHUNK) >= (c // CHUNK)
    for g in range(SG_GROUPS):
        wm = jnp.where(causal, sgw_ref[g], 0.0).astype(BF16)
        cols = slice(g * SG_BLOCK, (g + 1) * SG_BLOCK)
        for n in range(tm // SG_BLOCK):
            rows = slice(n * SG_BLOCK, (n + 1) * SG_BLOCK)
            mixed = jnp.dot(wm, sv_ref[rows, cols], preferred_element_type=F32) + sgb_ref[g]
            yb_ref[rows, cols] = (su_ref[rows, cols].astype(F32) * mixed).astype(BF16)
    a = jnp.dot(ya_ref[...], wa_ref[...], preferred_element_type=F32)
    bm = jnp.dot(yb_ref[...], wb_ref[...], preferred_element_type=F32)
    merged = ga_ref[...].astype(F32) * a + gb_ref[...].astype(F32) * bm
    h = x_ref[...] + jnp.dot(merged.astype(BF16), wo_ref[...], preferred_element_type=F32)
    h_ref[...] = h
    ms = jnp.mean(h * h, axis=-1, keepdims=True)
    xn = h * lax.rsqrt(ms + EPS) * g2_ref[...]
    xn_ref[...] = xn
    qp_ref[...] = jnp.dot(xn.astype(BF16), wq_ref[...], preferred_element_type=F32)


def _mix(proj, ya, x2, sgw, sgb, wa, wb, wo, g2, wq, tm=256):
    t = x2.shape[0]
    nq = wq.shape[1]
    full = lambda shape: pl.BlockSpec(shape, lambda i: (0,) * len(shape))
    col = lambda cb: pl.BlockSpec((tm, D_MODEL), lambda i: (i, cb))
    return pl.pallas_call(
        functools.partial(_mix_kernel, tm=tm),
        grid=(t // tm,),
        in_specs=[col(3), col(4), col(5), col(6), col(0), col(0),
                  full((SG_GROUPS, SG_BLOCK, SG_BLOCK)), full((SG_GROUPS, SG_BLOCK, 1)),
                  full((D_MODEL, D_MODEL)), full((D_MODEL, D_MODEL)), full((D_MODEL, D_MODEL)),
                  full((1, D_MODEL)), full((D_MODEL, nq))],
        out_specs=[col(0), col(0), pl.BlockSpec((tm, nq), lambda i: (i, 0))],
        out_shape=[jax.ShapeDtypeStruct((t, D_MODEL), F32), jax.ShapeDtypeStruct((t, D_MODEL), F32),
                   jax.ShapeDtypeStruct((t, nq), F32)],
        scratch_shapes=[pltpu.VMEM((tm, D_MODEL), BF16)],
        compiler_params=_params("parallel"),
        name="gmlp_merge",
    )(proj, proj, proj, proj, ya, x2, sgw, sgb, wa, wb, wo, g2, wq)


def _topk_kernel(qp_ref, sk_ref, idx_ref, gate_ref, sv_ref, si_ref, ts_ref, te_ref, *, tt):
    neg = -jnp.inf
    iota_k = lax.broadcasted_iota(I32, (N_KEYS, tt), 0).astype(F32)
    ncand = PEER_TOPK * PEER_TOPK
    iota_c = lax.broadcasted_iota(I32, (ncand, tt), 0).astype(F32)
    for h in range(PEER_HEADS):
        for p in range(2):
            c0 = (h * 2 + p) * D_KEY_HALF
            qh = qp_ref[:, c0:c0 + D_KEY_HALF].astype(BF16)
            s = lax.dot_general(sk_ref[p].astype(BF16), qh, (((1,), (1,)), ((), ())),
                                preferred_element_type=F32)
            for k in range(PEER_TOPK):
                m = jnp.max(s, axis=0, keepdims=True)
                ii = jnp.min(jnp.where(s == m, iota_k, float(N_KEYS)), axis=0, keepdims=True)
                s = jnp.where(iota_k == ii, neg, s)
                sv_ref[p, k:k + 1, :] = m
                si_ref[p, k:k + 1, :] = ii
        sv1 = sv_ref[1]
        si1 = si_ref[1]
        cand = jnp.concatenate([sv_ref[0, a:a + 1, :] + sv1 for a in range(PEER_TOPK)], axis=0)
        cidx = jnp.concatenate([si_ref[0, a:a + 1, :] * float(N_KEYS) + si1 for a in range(PEER_TOPK)],
                               axis=0)
        for k in range(PEER_TOPK):
            m = jnp.max(cand, axis=0, keepdims=True)
            pos = jnp.min(jnp.where(cand == m, iota_c, float(ncand)), axis=0, keepdims=True)
            sel = iota_c == pos
            e = jnp.max(jnp.where(sel, cidx, -1.0), axis=0, keepdims=True)
            cand = jnp.where(sel, neg, cand)
            ts_ref[h * PEER_TOPK + k:h * PEER_TOPK + k + 1, :] = m
            te_ref[h * PEER_TOPK + k:h * PEER_TOPK + k + 1, :] = e
        rows = slice(h * PEER_TOPK, (h + 1) * PEER_TOPK)
        ts = ts_ref[rows, :]
        ex = jnp.exp(ts - jnp.max(ts, axis=0, keepdims=True))
        gate_ref[rows, :] = ex / jnp.sum(ex, axis=0, keepdims=True)
    idx_ref[...] = (te_ref[...] * float(ROW_SUBLANES)).astype(I32)


def _topk(qp, subkeys, tt=LANES):
    t = qp.shape[0]
    nblk = t // tt
    out_spec = pl.BlockSpec((None, N_SEL, tt), lambda i: (i, 0, 0))
    return pl.pallas_call(
        functools.partial(_topk_kernel, tt=tt),
        grid=(nblk,),
        in_specs=[pl.BlockSpec((tt, qp.shape[1]), lambda i: (i, 0)),
                  pl.BlockSpec((2, N_KEYS, D_KEY_HALF), lambda i: (0, 0, 0))],
        out_specs=[out_spec, out_spec],
        out_shape=[jax.ShapeDtypeStruct((nblk, N_SEL, tt), I32),
                   jax.ShapeDtypeStruct((nblk, N_SEL, tt), F32)],
        scratch_shapes=[pltpu.VMEM((2, PEER_TOPK, tt), F32), pltpu.VMEM((2, PEER_TOPK, tt), F32),
                        pltpu.VMEM((N_SEL, tt), F32), pltpu.VMEM((N_SEL, tt), F32)],
        compiler_params=_params("parallel"),
        name="peer_topk",
    )(qp, subkeys)


def _gather_rows(idx_ref, tab_ref, tile_ref, t):
    for j in range(N_SEL):
        start = pl.multiple_of(idx_ref[j, t], ROW_SUBLANES)
        tile_ref[pl.ds(j, ROW_SUBLANES, stride=TILE_STRIDE), :] = tab_ref[pl.ds(start, ROW_SUBLANES), :]


def _unpack(tile_ref, s):
    g = tile_ref[s * TILE_STRIDE:s * TILE_STRIDE + N_SEL, :]
    lo = lax.bitcast_convert_type(lax.shift_left(g, 16), F32)
    hi = lax.bitcast_convert_type(g & HI_MASK, F32)
    return lo, hi


def _peer_u_kernel(idx_ref, x_ref, gate_ref, tab_ref, w_ref, tile_ref, *, tb):
    lane = lax.broadcasted_iota(I32, (N_SEL, tb), 1)

    def body(t, acc):
        _gather_rows(idx_ref, tab_ref, tile_ref, t)
        part = jnp.zeros((N_SEL, LANES), F32)
        for s in range(ROW_SUBLANES):
            lo, hi = _unpack(tile_ref, s)
            part = part + lo * x_ref[pl.ds(t, 1), s, :] + hi * x_ref[pl.ds(t, 1), ROW_SUBLANES + s, :]
        hid = jnp.sum(part, axis=1, keepdims=True)
        return jnp.where(lane == t, hid, acc)

    hid_t = lax.fori_loop(0, tb, body, jnp.zeros((N_SEL, tb), F32))
    w_ref[...] = gate_ref[...] * _gelu(hid_t)


def _peer_v_kernel(idx_ref, w_ref, tab_ref, y_ref, tile_ref, *, tb):
    lane = lax.broadcasted_iota(I32, (N_SEL, tb), 1)

    def body(t, carry):
        _gather_rows(idx_ref, tab_ref, tile_ref, t)
        wcol = jnp.sum(jnp.where(lane == t, w_ref[...], 0.0), axis=1, keepdims=True)
        for s in range(ROW_SUBLANES):
            lo, hi = _unpack(tile_ref, s)
            y_ref[pl.ds(t, 1), s, :] = jnp.sum(lo * wcol, axis=0, keepdims=True)
            y_ref[pl.ds(t, 1), ROW_SUBLANES + s, :] = jnp.sum(hi * wcol, axis=0, keepdims=True)
        return carry

    lax.fori_loop(0, tb, body, 0)


def _peer_specs(tb):
    idx_spec = pl.BlockSpec((None, N_SEL, tb), lambda i: (i, 0, 0), memory_space=pltpu.SMEM)
    blk_spec = pl.BlockSpec((None, N_SEL, tb), lambda i: (i, 0, 0))
    tok_spec = pl.BlockSpec((tb, 2 * ROW_SUBLANES, LANES), lambda i: (i, 0, 0))
    tab_spec = pl.BlockSpec(memory_space=pltpu.VMEM)
    tile = pltpu.VMEM((ROW_SUBLANES * TILE_STRIDE, LANES), I32)
    return idx_spec, blk_spec, tok_spec, tab_spec, tile


def _peer_u(idx, xn3, gate, tab, tb=LANES):
    nblk = idx.shape[0]
    idx_spec, blk_spec, tok_spec, tab_spec, tile = _peer_specs(tb)
    return pl.pallas_call(
        functools.partial(_peer_u_kernel, tb=tb),
        grid=(nblk,),
        in_specs=[idx_spec, tok_spec, blk_spec, tab_spec],
        out_specs=blk_spec,
        out_shape=jax.ShapeDtypeStruct((nblk, N_SEL, tb), F32),
        scratch_shapes=[tile],
        compiler_params=_params("arbitrary"),
        name="peer_u",
    )(idx, xn3, gate, tab)


def _peer_v(idx, w, tab, tb=LANES):
    nblk = idx.shape[0]
    idx_spec, blk_spec, tok_spec, tab_spec, tile = _peer_specs(tb)
    return pl.pallas_call(
        functools.partial(_peer_v_kernel, tb=tb),
        grid=(nblk,),
        in_specs=[idx_spec, blk_spec, tab_spec],
        out_specs=tok_spec,
        out_shape=jax.ShapeDtypeStruct((nblk * tb, 2 * ROW_SUBLANES, LANES), F32),
        scratch_shapes=[tile],
        compiler_params=_params("arbitrary"),
        name="peer_v",
    )(idx, w, tab)


def _final_kernel(h_ref, y_ref, g_ref, o_ref):
    h = h_ref[...] + y_ref[...]
    ms = jnp.mean(h * h, axis=-1, keepdims=True)
    o_ref[...] = h * lax.rsqrt(ms + EPS) * g_ref[...]


def _final(h, y, g, tm=1024):
    t = h.shape[0]
    tm = min(tm, t)
    spec = pl.BlockSpec((tm, D_MODEL), lambda i: (i, 0))
    return pl.pallas_call(
        _final_kernel,
        grid=(t // tm,),
        in_specs=[spec, spec, pl.BlockSpec((1, D_MODEL), lambda i: (0, 0))],
        out_specs=spec,
        out_shape=jax.ShapeDtypeStruct((t, D_MODEL), F32),
        compiler_params=_params("parallel"),
        name="final_norm",
    )(h, y, g)


def _head_major_columns():
    h = jnp.arange(DA_HEADS)[:, None, None]
    m = jnp.arange(2)[None, :, None]
    d = jnp.arange(DA_DQK)[None, None, :]
    return (m * (DA_HEADS * DA_DQK) + h * DA_DQK + d).reshape(-1)


def kernel(x, norm1_g, w_in, lambda_q1, lambda_k1, lambda_q2, lambda_k2, da_subln_g, sg_ln_g, sg_ln_b,
           sg_w, sg_b, w_branch_attn, w_branch_sg, w_out, norm2_g, peer_w_query, peer_subkeys, peer_u,
           peer_v, final_g):
    b, s, d = x.shape
    t = b * s
    assert w_in.shape[0] == 1, "single-layer block"
    x2 = x.reshape(t, d)
    row = lambda a: a.reshape(1, -1)

    perm = _head_major_columns()
    qk = 2 * DA_HEADS * DA_DQK
    w = w_in[0]
    w = jnp.concatenate([w[:, :qk][:, perm], w[:, qk:2 * qk][:, perm], w[:, 2 * qk:]], axis=1).astype(BF16)

    proj = _inproj(x2, row(norm1_g[0]), w, row(sg_ln_g[0]), row(sg_ln_b[0]))
    ya = _attention(proj.reshape(b, s, -1), row(lambda_q1[0]), row(lambda_k1[0]), row(lambda_q2[0]),
                    row(lambda_k2[0]), row(da_subln_g[0]))
    h, xn, qp = _mix(proj, ya.reshape(t, -1), x2, sg_w[0], sg_b[0][:, :, None],
                     w_branch_attn[0].astype(BF16), w_branch_sg[0].astype(BF16), w_out[0].astype(BF16),
                     row(norm2_g[0]), peer_w_query[0].astype(BF16))
    idx, gate = _topk(qp, peer_subkeys[0])
    wts = _peer_u(idx, xn.reshape(t, 2 * ROW_SUBLANES, LANES), gate, _pack_table(peer_u[0]))
    y = _peer_v(idx, wts, _pack_table(peer_v[0]))
    out = _final(h, y.reshape(t, d), row(final_g))
    return out.reshape(b, s, d)
```

```python
import functools
import math

import jax
import jax.numpy as jnp
from jax import lax
from jax.experimental import pallas as pl
from jax.experimental.pallas import tpu as pltpu

F32 = jnp.float32
BF16 = jnp.bfloat16
I32 = jnp.int32
U32 = jnp.uint32

D_MODEL = 1024
CHUNK = 64
EPS = 1e-6
DA_HEADS = 8
DA_DQK = 64
DA_DV = 128
SG_BLOCK = 128
SG_GROUPS = 8
PEER_HEADS = 8
N_KEYS = 128
PEER_TOPK = 16
D_KEY_HALF = 128
N_SEL = PEER_HEADS * PEER_TOPK
LANES = 128
WORDS_PER_ROW = D_MODEL // 2
ROW_SUBLANES = WORDS_PER_ROW // LANES
TILE_STRIDE = N_SEL + 8
VMEM_LIMIT = 56 * 1024 * 1024
LAMBDA_INIT = 0.8 - 0.6 * math.exp(-0.3 * 0)
SQRT_HALF = 0.7071067811865476
QK_SCALE = DA_DQK ** -0.5 * math.log2(math.e)
ONES_ROWS = 16


def _gelu(x):
    return 0.5 * x * (1.0 + lax.erf(x * SQRT_HALF))


def _params(*sem):
    return pltpu.CompilerParams(dimension_semantics=sem, vmem_limit_bytes=VMEM_LIMIT)


def _pack_kernel(t_ref, o_ref):
    t = t_ref[...]
    o_ref[...] = pltpu.pack_elementwise([t[:, :WORDS_PER_ROW], t[:, WORDS_PER_ROW:]], packed_dtype=BF16)


def _pack_table(tab, rows=512):
    n = tab.shape[0]
    out = pl.pallas_call(
        _pack_kernel,
        grid=(n // rows,),
        in_specs=[pl.BlockSpec((rows, D_MODEL), lambda i: (i, 0))],
        out_specs=pl.BlockSpec((rows, WORDS_PER_ROW), lambda i: (i, 0)),
        out_shape=jax.ShapeDtypeStruct((n, WORDS_PER_ROW), U32),
        compiler_params=_params("parallel"),
        name="pack_table",
    )(tab)
    return out.reshape(n * ROW_SUBLANES, LANES)


def _inproj_kernel(x_ref, g_ref, w_ref, lng_ref, lnb_ref, o_ref, xn_ref):
    j = pl.program_id(1)

    @pl.when(j == 0)
    def _():
        x = x_ref[...]
        ms = jnp.mean(x * x, axis=-1, keepdims=True)
        xn_ref[...] = (x * lax.rsqrt(ms + EPS) * g_ref[...]).astype(BF16)

    acc = jnp.dot(xn_ref[...], w_ref[...], preferred_element_type=F32)

    @pl.when(j == 0)
    def _():
        o_ref[...] = (acc * QK_SCALE).astype(BF16)

    @pl.when((j == 1) | (j == 2))
    def _():
        o_ref[...] = acc.astype(BF16)

    @pl.when(j == 3)
    def _():
        o_ref[...] = _gelu(acc).astype(BF16)

    @pl.when(j == 4)
    def _():
        v = _gelu(acc)
        mu = jnp.mean(v, axis=-1, keepdims=True)
        var = jnp.mean(jnp.square(v - mu), axis=-1, keepdims=True)
        o_ref[...] = ((v - mu) * lax.rsqrt(var + EPS) * lng_ref[...] + lnb_ref[...]).astype(BF16)

    @pl.when(j >= 5)
    def _():
        o_ref[...] = (1.0 / (1.0 + jnp.exp(-acc))).astype(BF16)


def _inproj(x2, g1, w, lng, lnb, tm=512):
    t = x2.shape[0]
    ncol = w.shape[1] // D_MODEL
    return pl.pallas_call(
        _inproj_kernel,
        grid=(t // tm, ncol),
        in_specs=[
            pl.BlockSpec((tm, D_MODEL), lambda i, j: (i, 0)),
            pl.BlockSpec((1, D_MODEL), lambda i, j: (0, 0)),
            pl.BlockSpec((D_MODEL, D_MODEL), lambda i, j: (0, j)),
            pl.BlockSpec((1, D_MODEL), lambda i, j: (0, 0)),
            pl.BlockSpec((1, D_MODEL), lambda i, j: (0, 0)),
        ],
        out_specs=pl.BlockSpec((tm, D_MODEL), lambda i, j: (i, j)),
        out_shape=jax.ShapeDtypeStruct((t, w.shape[1]), BF16),
        scratch_shapes=[pltpu.VMEM((tm, D_MODEL), BF16)],
        compiler_params=_params("parallel", "arbitrary"),
        name="inproj",
    )(x2, g1, w, lng, lnb)


def _attn_kernel(q_ref, k_ref, v_ref, lq1_ref, lk1_ref, lq2_ref, lk2_ref, sg_ref, o_ref,
                 qbd_ref, vt_ref, acc_ref, pt_ref, *, tq, hp):
    i = pl.program_id(2)
    nkv = vt_ref.shape[1]
    hcols = lambda h: slice(h * LANES, (h + 1) * LANES)

    @pl.when(i == 0)
    def _():
        for h in range(hp):
            for j in range(nkv):
                vt_ref[h, j, :DA_DV, :] = v_ref[j * tq:(j + 1) * tq, hcols(h)].T
                vt_ref[h, j, DA_DV:, :] = jnp.ones((ONES_ROWS, tq), BF16)

    lane = lax.broadcasted_iota(I32, (tq, LANES), 1)
    for h in range(hp):
        q = q_ref[:, hcols(h)]
        zero = jnp.zeros_like(q)
        qbd_ref[h] = jnp.concatenate([jnp.where(lane < DA_DQK, q, zero), jnp.where(lane >= DA_DQK, q, zero)],
                                     axis=0)
        acc_ref[h] = jnp.zeros(acc_ref.shape[1:], F32)
        pt_ref[h] = jnp.zeros(pt_ref.shape[1:], BF16)

    def value_update(j, alphas):
        pvs = [jnp.dot(vt_ref[h, j], pt_ref[h], preferred_element_type=F32) for h in range(hp)]
        for h in range(hp):
            acc_ref[h] = alphas[h] * acc_ref[h] + pvs[h]

    def step(j, carry, mask):
        ms, alphas = carry
        kstart = pl.multiple_of(j * tq, tq)
        jp = jnp.maximum(j - 1, 0)
        sts, pvs = [], []
        for h in range(hp):
            sts.append(lax.dot_general(k_ref[pl.ds(kstart, tq), hcols(h)], qbd_ref[h], (((1,), (1,)), ((), ())),
                                       preferred_element_type=F32))
            pvs.append(jnp.dot(vt_ref[h, jp], pt_ref[h], preferred_element_type=F32))
        new_ms, new_alphas = [], []
        for h in range(hp):
            st = sts[h] if mask is None else jnp.where(mask, sts[h], -1e30)
            m_new = jnp.maximum(ms[h], jnp.max(st, axis=0, keepdims=True))
            new_alphas.append(jnp.exp2(ms[h] - m_new))
            new_ms.append(m_new)
            acc_ref[h] = alphas[h] * acc_ref[h] + pvs[h]
            pt_ref[h] = jnp.exp2(st - m_new).astype(BF16)
        return tuple(new_ms), tuple(new_alphas)

    init = (tuple(jnp.full((1, 2 * tq), -jnp.inf, F32) for _ in range(hp)),
            tuple(jnp.ones((1, 2 * tq), F32) for _ in range(hp)))
    carry = lax.fori_loop(0, i, lambda j, c: step(j, c, None), init)
    key = lax.broadcasted_iota(I32, (tq, 2 * tq), 0)
    col = lax.broadcasted_iota(I32, (tq, 2 * tq), 1)
    qpos = jnp.where(col >= tq, col - tq, col)
    _, alphas = step(i, carry, (key // CHUNK) <= (qpos // CHUNK))
    value_update(i, alphas)

    lam = (jnp.exp(jnp.sum(lq1_ref[...] * lk1_ref[...], axis=-1, keepdims=True))
           - jnp.exp(jnp.sum(lq2_ref[...] * lk2_ref[...], axis=-1, keepdims=True)) + LAMBDA_INIT)
    for h in range(hp):
        acc = acc_ref[h]
        on = acc[:DA_DV] / acc[DA_DV:DA_DV + 1]
        o = on[:, :tq] - lam * on[:, tq:]
        msq = jnp.mean(o * o, axis=0, keepdims=True)
        o = o * lax.rsqrt(msq + EPS) * sg_ref[...] * (1.0 - LAMBDA_INIT)
        o_ref[:, hcols(h)] = o.T.astype(BF16)


def _attention(proj3, lq1, lk1, lq2, lk2, subln_col, tq=256, hp=4):
    b, s, _ = proj3.shape
    ng = DA_HEADS // hp
    lam_spec = pl.BlockSpec((1, DA_DQK), lambda bi, g, i: (0, 0))
    return pl.pallas_call(
        functools.partial(_attn_kernel, tq=tq, hp=hp),
        grid=(b, ng, s // tq),
        in_specs=[
            pl.BlockSpec((None, tq, hp * LANES), lambda bi, g, i: (bi, i, g)),
            pl.BlockSpec((None, s, hp * LANES), lambda bi, g, i: (bi, 0, ng + g)),
            pl.BlockSpec((None, s, hp * LANES), lambda bi, g, i: (bi, 0, 2 * ng + g)),
            lam_spec, lam_spec, lam_spec, lam_spec,
            pl.BlockSpec((DA_DV, 1), lambda bi, g, i: (0, 0)),
        ],
        out_specs=pl.BlockSpec((None, tq, hp * DA_DV), lambda bi, g, i: (bi, i, g)),
        out_shape=jax.ShapeDtypeStruct((b, s, DA_HEADS * DA_DV), BF16),
        scratch_shapes=[pltpu.VMEM((hp, 2 * tq, LANES), BF16),
                        pltpu.VMEM((hp, s // tq, DA_DV + ONES_ROWS, tq), BF16),
                        pltpu.VMEM((hp, DA_DV + ONES_ROWS, 2 * tq), F32),
                        pltpu.VMEM((hp, tq, 2 * tq), BF16)],
        compiler_params=_params("parallel", "parallel", "arbitrary"),
        name="diff_attn",
    )(proj3, proj3, proj3, lq1, lk1, lq2, lk2, subln_col)


def _mix_kernel(su_ref, sv_ref, ga_ref, gb_ref, ya_ref, x_ref, sgw_ref, sgb_ref, wa_ref, wb_ref,
                wo_ref, g2_ref, wq_ref, h_ref, xn_ref, qp_ref, yb_ref, *, tm):
    r = lax.broadcasted_iota(I32, (SG_BLOCK, SG_BLOCK), 0)
    c = lax.broadcasted_iota(I32, (SG_BLOCK, SG_BLOCK), 1)
    causal = (r // CHUNK) >= (c // CHUNK)
    for g in range(SG_GROUPS):
        wm = jnp.where(causal, sgw_ref[g], 0.0).astype(BF16)
        cols = slice(g * SG_BLOCK, (g + 1) * SG_BLOCK)
        for n in range(tm // SG_BLOCK):
            rows = slice(n * SG_BLOCK, (n + 1) * SG_BLOCK)
            mixed = jnp.dot(wm, sv_ref[rows, cols], preferred_element_type=F32) + sgb_ref[g]
            yb_ref[rows, cols] = (su_ref[rows, cols].astype(F32) * mixed).astype(BF16)
    a = jnp.dot(ya_ref[...], wa_ref[...], preferred_element_type=F32)
    bm = jnp.dot(yb_ref[...], wb_ref[...], preferred_element_type=F32)
    merged = ga_ref[...].astype(F32) * a + gb_ref[...].astype(F32) * bm
    h = x_ref[...] + jnp.dot(merged.astype(BF16), wo_ref[...], preferred_element_type=F32)
    h_ref[...] = h
    ms = jnp.mean(h * h, axis=-1, keepdims=True)
    xn = h * lax.rsqrt(ms + EPS) * g2_ref[...]
    xn_ref[...] = xn
    qp_ref[...] = jnp.dot(xn.astype(BF16), wq_ref[...], preferred_element_type=F32)


def _mix(proj, ya, x2, sgw, sgb, wa, wb, wo, g2, wq, tm=256):
    t = x2.shape[0]
    nq = wq.shape[1]
    full = lambda shape: pl.BlockSpec(shape, lambda i: (0,) * len(shape))
    col = lambda cb: pl.BlockSpec((tm, D_MODEL), lambda i: (i, cb))
    return pl.pallas_call(
        functools.partial(_mix_kernel, tm=tm),
        grid=(t // tm,),
        in_specs=[col(3), col(4), col(5), col(6), col(0), col(0),
                  full((SG_GROUPS, SG_BLOCK, SG_BLOCK)), full((SG_GROUPS, SG_BLOCK, 1)),
                  full((D_MODEL, D_MODEL)), full((D_MODEL, D_MODEL)), full((D_MODEL, D_MODEL)),
                  full((1, D_MODEL)), full((D_MODEL, nq))],
        out_specs=[col(0), col(0), pl.BlockSpec((tm, nq), lambda i: (i, 0))],
        out_shape=[jax.ShapeDtypeStruct((t, D_MODEL), F32), jax.ShapeDtypeStruct((t, D_MODEL), F32),
                   jax.ShapeDtypeStruct((t, nq), F32)],
        scratch_shapes=[pltpu.VMEM((tm, D_MODEL), BF16)],
        compiler_params=_params("parallel"),
        name="gmlp_merge",
    )(proj, proj, proj, proj, ya, x2, sgw, sgb, wa, wb, wo, g2, wq)


def _topk_kernel(qp_ref, sk_ref, idx_ref, gate_ref, sv_ref, si_ref, ts_ref, te_ref, *, tt):
    neg = -jnp.inf
    iota_k = lax.broadcasted_iota(I32, (N_KEYS, tt), 0).astype(F32)
    ncand = PEER_TOPK * PEER_TOPK
    iota_c = lax.broadcasted_iota(I32, (ncand, tt), 0).astype(F32)
    for h in range(PEER_HEADS):
        for p in range(2):
            c0 = (h * 2 + p) * D_KEY_HALF
            qh = qp_ref[:, c0:c0 + D_KEY_HALF].astype(BF16)
            s = lax.dot_general(sk_ref[p].astype(BF16), qh, (((1,), (1,)), ((), ())),
                                preferred_element_type=F32)
            for k in range(PEER_TOPK):
                m = jnp.max(s, axis=0, keepdims=True)
                ii = jnp.min(jnp.where(s == m, iota_k, float(N_KEYS)), axis=0, keepdims=True)
                s = jnp.where(iota_k == ii, neg, s)
                sv_ref[p, k:k + 1, :] = m
                si_ref[p, k:k + 1, :] = ii
        sv1 = sv_ref[1]
        si1 = si_ref[1]
        cand = jnp.concatenate([sv_ref[0, a:a + 1, :] + sv1 for a in range(PEER_TOPK)], axis=0)
        cidx = jnp.concatenate([si_ref[0, a:a + 1, :] * float(N_KEYS) + si1 for a in range(PEER_TOPK)],
                               axis=0)
        for k in range(PEER_TOPK):
            m = jnp.max(cand, axis=0, keepdims=True)
            pos = jnp.min(jnp.where(cand == m, iota_c, float(ncand)), axis=0, keepdims=True)
            sel = iota_c == pos
            e = jnp.max(jnp.where(sel, cidx, -1.0), axis=0, keepdims=True)
            cand = jnp.where(sel, neg, cand)
            ts_ref[h * PEER_TOPK + k:h * PEER_TOPK + k + 1, :] = m
            te_ref[h * PEER_TOPK + k:h * PEER_TOPK + k + 1, :] = e
        rows = slice(h * PEER_TOPK, (h + 1) * PEER_TOPK)
        ts = ts_ref[rows, :]
        ex = jnp.exp(ts - jnp.max(ts, axis=0, keepdims=True))
        gate_ref[rows, :] = ex / jnp.sum(ex, axis=0, keepdims=True)
    idx_ref[...] = (te_ref[...] * float(ROW_SUBLANES)).T.astype(I32)


def _topk(qp, subkeys, tt=LANES):
    t = qp.shape[0]
    nblk = t // tt
    out_spec = pl.BlockSpec((None, N_SEL, tt), lambda i: (i, 0, 0))
    return pl.pallas_call(
        functools.partial(_topk_kernel, tt=tt),
        grid=(nblk,),
        in_specs=[pl.BlockSpec((tt, qp.shape[1]), lambda i: (i, 0)),
                  pl.BlockSpec((2, N_KEYS, D_KEY_HALF), lambda i: (0, 0, 0))],
        out_specs=[pl.BlockSpec((None, tt, N_SEL), lambda i: (i, 0, 0)), out_spec],
        out_shape=[jax.ShapeDtypeStruct((nblk, tt, N_SEL), I32),
                   jax.ShapeDtypeStruct((nblk, N_SEL, tt), F32)],
        scratch_shapes=[pltpu.VMEM((2, PEER_TOPK, tt), F32), pltpu.VMEM((2, PEER_TOPK, tt), F32),
                        pltpu.VMEM((N_SEL, tt), F32), pltpu.VMEM((N_SEL, tt), F32)],
        compiler_params=_params("parallel"),
        name="peer_topk",
    )(qp, subkeys)


def _gather_rows(idx_ref, tab_ref, tile_ref, t):
    for j in range(N_SEL):
        start = pl.multiple_of(idx_ref[t, j], ROW_SUBLANES)
        tile_ref[pl.ds(j, ROW_SUBLANES, stride=TILE_STRIDE), :] = tab_ref[pl.ds(start, ROW_SUBLANES), :]


def _unpack(tile_ref, s):
    g = tile_ref[s * TILE_STRIDE:s * TILE_STRIDE + N_SEL, :]
    lo = pltpu.unpack_elementwise(g, index=0, packed_dtype=BF16, unpacked_dtype=F32)
    hi = pltpu.unpack_elementwise(g, index=1, packed_dtype=BF16, unpacked_dtype=F32)
    return lo, hi


def _next_token(t, tb):
    return jnp.minimum(t, tb - 1)


def _peer_u_kernel(idx_ref, x_ref, gate_ref, tab_ref, w_ref, tile_a, tile_b, hid_ref, part_ref, *, tb):
    lane = lax.broadcasted_iota(I32, (N_SEL, tb), 1)

    def products(tile_ref, t, slot):
        part = jnp.zeros((N_SEL, LANES), F32)
        for s in range(ROW_SUBLANES):
            lo, hi = _unpack(tile_ref, s)
            part = part + lo * x_ref[pl.ds(t, 1), s, :] + hi * x_ref[pl.ds(t, 1), ROW_SUBLANES + s, :]
        part_ref[slot] = part

    def insert(slot, t):
        col = jnp.sum(part_ref[slot], axis=1, keepdims=True)
        hid_ref[...] = jnp.where(lane == t, col, hid_ref[...])

    part_ref[...] = jnp.zeros(part_ref.shape, F32)
    hid_ref[...] = jnp.zeros(hid_ref.shape, F32)
    _gather_rows(idx_ref, tab_ref, tile_a, 0)

    def body(p, carry):
        t = 2 * p
        insert(0, t - 2)
        insert(1, t - 1)
        _gather_rows(idx_ref, tab_ref, tile_b, t + 1)
        products(tile_a, t, 0)
        _gather_rows(idx_ref, tab_ref, tile_a, _next_token(t + 2, tb))
        products(tile_b, t + 1, 1)
        return carry

    lax.fori_loop(0, tb // 2, body, 0)
    insert(0, tb - 2)
    insert(1, tb - 1)
    w_ref[...] = gate_ref[...] * _gelu(hid_ref[...])


def _peer_v_kernel(idx_ref, w_ref, tab_ref, y_ref, tile_a, tile_b, *, tb):
    lane = lax.broadcasted_iota(I32, (N_SEL, tb), 1)

    def weighted_sum(tile_ref, t):
        wcol = jnp.sum(jnp.where(lane == t, w_ref[...], 0.0), axis=1, keepdims=True)
        for s in range(ROW_SUBLANES):
            lo, hi = _unpack(tile_ref, s)
            y_ref[pl.ds(t, 1), s, :] = jnp.sum(lo * wcol, axis=0, keepdims=True)
            y_ref[pl.ds(t, 1), ROW_SUBLANES + s, :] = jnp.sum(hi * wcol, axis=0, keepdims=True)

    _gather_rows(idx_ref, tab_ref, tile_a, 0)

    def body(p, carry):
        t = 2 * p
        _gather_rows(idx_ref, tab_ref, tile_b, t + 1)
        weighted_sum(tile_a, t)
        _gather_rows(idx_ref, tab_ref, tile_a, _next_token(t + 2, tb))
        weighted_sum(tile_b, t + 1)
        return carry

    lax.fori_loop(0, tb // 2, body, 0)


def _peer_specs(tb):
    idx_spec = pl.BlockSpec((None, tb, N_SEL), lambda i: (i, 0, 0), memory_space=pltpu.SMEM)
    blk_spec = pl.BlockSpec((None, N_SEL, tb), lambda i: (i, 0, 0))
    tok_spec = pl.BlockSpec((tb, 2 * ROW_SUBLANES, LANES), lambda i: (i, 0, 0))
    tab_spec = pl.BlockSpec(memory_space=pltpu.VMEM)
    tile = pltpu.VMEM((ROW_SUBLANES * TILE_STRIDE, LANES), U32)
    return idx_spec, blk_spec, tok_spec, tab_spec, [tile, tile]


def _peer_u(idx, xn3, gate, tab, tb=LANES):
    nblk = idx.shape[0]
    idx_spec, blk_spec, tok_spec, tab_spec, tiles = _peer_specs(tb)
    return pl.pallas_call(
        functools.partial(_peer_u_kernel, tb=tb),
        grid=(nblk,),
        in_specs=[idx_spec, tok_spec, blk_spec, tab_spec],
        out_specs=blk_spec,
        out_shape=jax.ShapeDtypeStruct((nblk, N_SEL, tb), F32),
        scratch_shapes=tiles + [pltpu.VMEM((N_SEL, tb), F32), pltpu.VMEM((2, N_SEL, LANES), F32)],
        compiler_params=_params("arbitrary"),
        name="peer_u",
    )(idx, xn3, gate, tab)


def _peer_v(idx, w, tab, tb=LANES):
    nblk = idx.shape[0]
    idx_spec, blk_spec, tok_spec, tab_spec, tiles = _peer_specs(tb)
    return pl.pallas_call(
        functools.partial(_peer_v_kernel, tb=tb),
        grid=(nblk,),
        in_specs=[idx_spec, blk_spec, tab_spec],
        out_specs=tok_spec,
        out_shape=jax.ShapeDtypeStruct((nblk * tb, 2 * ROW_SUBLANES, LANES), F32),
        scratch_shapes=tiles,
        compiler_params=_params("arbitrary"),
        name="peer_v",
    )(idx, w, tab)


def _final_kernel(h_ref, y_ref, g_ref, o_ref):
    h = h_ref[...] + y_ref[...]
    ms = jnp.mean(h * h, axis=-1, keepdims=True)
    o_ref[...] = h * lax.rsqrt(ms + EPS) * g_ref[...]


def _final(h, y, g, tm=1024):
    t = h.shape[0]
    tm = min(tm, t)
    spec = pl.BlockSpec((tm, D_MODEL), lambda i: (i, 0))
    return pl.pallas_call(
        _final_kernel,
        grid=(t // tm,),
        in_specs=[spec, spec, pl.BlockSpec((1, D_MODEL), lambda i: (0, 0))],
        out_specs=spec,
        out_shape=jax.ShapeDtypeStruct((t, D_MODEL), F32),
        compiler_params=_params("parallel"),
        name="final_norm",
    )(h, y, g)


def _head_major_columns():
    h = jnp.arange(DA_HEADS)[:, None, None]
    m = jnp.arange(2)[None, :, None]
    d = jnp.arange(DA_DQK)[None, None, :]
    return (m * (DA_HEADS * DA_DQK) + h * DA_DQK + d).reshape(-1)


def kernel(x, norm1_g, w_in, lambda_q1, lambda_k1, lambda_q2, lambda_k2, da_subln_g, sg_ln_g, sg_ln_b,
           sg_w, sg_b, w_branch_attn, w_branch_sg, w_out, norm2_g, peer_w_query, peer_subkeys, peer_u,
           peer_v, final_g):
    b, s, d = x.shape
    t = b * s
    assert w_in.shape[0] == 1, "single-layer block"
    x2 = x.reshape(t, d)
    row = lambda a: a.reshape(1, -1)

    perm = _head_major_columns()
    qk = 2 * DA_HEADS * DA_DQK
    w = w_in[0]
    w = jnp.concatenate([w[:, :qk][:, perm], w[:, qk:2 * qk][:, perm], w[:, 2 * qk:]], axis=1).astype(BF16)

    proj = _inproj(x2, row(norm1_g[0]), w, row(sg_ln_g[0]), row(sg_ln_b[0]))
    ya = _attention(proj.reshape(b, s, -1), row(lambda_q1[0]), row(lambda_k1[0]), row(lambda_q2[0]),
                    row(lambda_k2[0]), da_subln_g[0].reshape(-1, 1))
    h, xn, qp = _mix(proj, ya.reshape(t, -1), x2, sg_w[0], sg_b[0][:, :, None],
                     w_branch_attn[0].astype(BF16), w_branch_sg[0].astype(BF16), w_out[0].astype(BF16),
                     row(norm2_g[0]), peer_w_query[0].astype(BF16))
    idx, gate = _topk(qp, peer_subkeys[0])
    wts = _peer_u(idx, xn.reshape(t, 2 * ROW_SUBLANES, LANES), gate, _pack_table(peer_u[0]))
    y = _peer_v(idx, wts, _pack_table(peer_v[0]))
    out = _final(h, y.reshape(t, d), row(final_g))
    return out.reshape(b, s, d)
```

```python
import functools
import math

import jax
import jax.numpy as jnp
from jax import lax
from jax.experimental import pallas as pl
from jax.experimental.pallas import tpu as pltpu

F32 = jnp.float32
BF16 = jnp.bfloat16
I32 = jnp.int32
U32 = jnp.uint32

D_MODEL = 1024
CHUNK = 64
EPS = 1e-6
DA_HEADS = 8
DA_DQK = 64
DA_DV = 128
SG_BLOCK = 128
SG_GROUPS = 8
PEER_HEADS = 8
N_KEYS = 128
PEER_TOPK = 16
D_KEY_HALF = 128
N_SEL = PEER_HEADS * PEER_TOPK
LANES = 128
WORDS_PER_ROW = D_MODEL // 2
ROW_SUBLANES = WORDS_PER_ROW // LANES
TILE_STRIDE = N_SEL + 8
VMEM_LIMIT = 56 * 1024 * 1024
LAMBDA_INIT = 0.8 - 0.6 * math.exp(-0.3 * 0)
SQRT_HALF = 0.7071067811865476
QK_SCALE = DA_DQK ** -0.5 * math.log2(math.e)
ONES_ROWS = 16


def _gelu(x):
    return 0.5 * x * (1.0 + lax.erf(x * SQRT_HALF))


def _params(*sem):
    return pltpu.CompilerParams(dimension_semantics=sem, vmem_limit_bytes=VMEM_LIMIT)


def _pack_kernel(t_ref, o_ref):
    t = t_ref[...]
    o_ref[...] = pltpu.pack_elementwise([t[:, :WORDS_PER_ROW], t[:, WORDS_PER_ROW:]], packed_dtype=BF16)


def _pack_table(tab, rows=512):
    n = tab.shape[0]
    out = pl.pallas_call(
        _pack_kernel,
        grid=(n // rows,),
        in_specs=[pl.BlockSpec((rows, D_MODEL), lambda i: (i, 0))],
        out_specs=pl.BlockSpec((rows, WORDS_PER_ROW), lambda i: (i, 0)),
        out_shape=jax.ShapeDtypeStruct((n, WORDS_PER_ROW), U32),
        compiler_params=_params("parallel"),
        name="pack_table",
    )(tab)
    return out.reshape(n * ROW_SUBLANES, LANES)


def _inproj_kernel(x_ref, g_ref, w_ref, lng_ref, lnb_ref, o_ref):
    x = x_ref[...]
    ms = jnp.mean(x * x, axis=-1, keepdims=True)
    xn = (x * lax.rsqrt(ms + EPS) * g_ref[...]).astype(BF16)

    def block(j):
        cols = slice(j * D_MODEL, (j + 1) * D_MODEL)
        return cols, jnp.dot(xn, w_ref[:, cols], preferred_element_type=F32)

    cols, acc = block(0)
    o_ref[:, cols] = (acc * QK_SCALE).astype(BF16)
    for j in (1, 2):
        cols, acc = block(j)
        o_ref[:, cols] = acc.astype(BF16)
    cols, acc = block(3)
    o_ref[:, cols] = _gelu(acc).astype(BF16)
    cols, acc = block(4)
    v = _gelu(acc)
    mu = jnp.mean(v, axis=-1, keepdims=True)
    var = jnp.mean(jnp.square(v - mu), axis=-1, keepdims=True)
    o_ref[:, cols] = ((v - mu) * lax.rsqrt(var + EPS) * lng_ref[...] + lnb_ref[...]).astype(BF16)
    for j in (5, 6):
        cols, acc = block(j)
        o_ref[:, cols] = (1.0 / (1.0 + jnp.exp(-acc))).astype(BF16)


def _inproj(x2, g1, w, lng, lnb, tm=256):
    t = x2.shape[0]
    assert w.shape[1] == 7 * D_MODEL
    vec = pl.BlockSpec((1, D_MODEL), lambda i: (0, 0))
    return pl.pallas_call(
        _inproj_kernel,
        grid=(t // tm,),
        in_specs=[pl.BlockSpec((tm, D_MODEL), lambda i: (i, 0)), vec,
                  pl.BlockSpec(w.shape, lambda i: (0, 0), pipeline_mode=pl.Buffered(1)), vec, vec],
        out_specs=pl.BlockSpec((tm, w.shape[1]), lambda i: (i, 0)),
        out_shape=jax.ShapeDtypeStruct((t, w.shape[1]), BF16),
        compiler_params=_params("parallel"),
        name="inproj",
    )(x2, g1, w, lng, lnb)


def _attn_kernel(q_ref, k_ref, v_ref, lq1_ref, lk1_ref, lq2_ref, lk2_ref, sg_ref, o_ref,
                 qbd_ref, vt_ref, acc_ref, pt_ref, *, tq, hp):
    i = pl.program_id(2)
    nkv = vt_ref.shape[1]
    hcols = lambda h: slice(h * LANES, (h + 1) * LANES)

    @pl.when(i == 0)
    def _():
        for h in range(hp):
            for j in range(nkv):
                vt_ref[h, j, :DA_DV, :] = v_ref[j * tq:(j + 1) * tq, hcols(h)].T
                vt_ref[h, j, DA_DV:, :] = jnp.ones((ONES_ROWS, tq), BF16)

    lane = lax.broadcasted_iota(I32, (tq, LANES), 1)
    for h in range(hp):
        q = q_ref[:, hcols(h)]
        zero = jnp.zeros_like(q)
        qbd_ref[h] = jnp.concatenate([jnp.where(lane < DA_DQK, q, zero), jnp.where(lane >= DA_DQK, q, zero)],
                                     axis=0)
        acc_ref[h] = jnp.zeros(acc_ref.shape[1:], F32)
        pt_ref[h] = jnp.zeros(pt_ref.shape[1:], BF16)

    def value_update(j, alphas):
        pvs = [jnp.dot(vt_ref[h, j], pt_ref[h], preferred_element_type=F32) for h in range(hp)]
        for h in range(hp):
            acc_ref[h] = alphas[h] * acc_ref[h] + pvs[h]

    def step(j, carry, mask):
        ms, alphas = carry
        kstart = pl.multiple_of(j * tq, tq)
        jp = jnp.maximum(j - 1, 0)
        sts, pvs = [], []
        for h in range(hp):
            sts.append(lax.dot_general(k_ref[pl.ds(kstart, tq), hcols(h)], qbd_ref[h], (((1,), (1,)), ((), ())),
                                       preferred_element_type=F32))
            pvs.append(jnp.dot(vt_ref[h, jp], pt_ref[h], preferred_element_type=F32))
        new_ms, new_alphas = [], []
        for h in range(hp):
            st = sts[h] if mask is None else jnp.where(mask, sts[h], -1e30)
            m_new = jnp.maximum(ms[h], jnp.max(st, axis=0, keepdims=True))
            new_alphas.append(jnp.exp2(ms[h] - m_new))
            new_ms.append(m_new)
            acc_ref[h] = alphas[h] * acc_ref[h] + pvs[h]
            pt_ref[h] = jnp.exp2(st - m_new).astype(BF16)
        return tuple(new_ms), tuple(new_alphas)

    init = (tuple(jnp.full((1, 2 * tq), -jnp.inf, F32) for _ in range(hp)),
            tuple(jnp.ones((1, 2 * tq), F32) for _ in range(hp)))
    carry = lax.fori_loop(0, i, lambda j, c: step(j, c, None), init)
    key = lax.broadcasted_iota(I32, (tq, 2 * tq), 0)
    col = lax.broadcasted_iota(I32, (tq, 2 * tq), 1)
    qpos = jnp.where(col >= tq, col - tq, col)
    _, alphas = step(i, carry, (key // CHUNK) <= (qpos // CHUNK))
    value_update(i, alphas)

    lam = (jnp.exp(jnp.sum(lq1_ref[...] * lk1_ref[...], axis=-1, keepdims=True))
           - jnp.exp(jnp.sum(lq2_ref[...] * lk2_ref[...], axis=-1, keepdims=True)) + LAMBDA_INIT)
    for h in range(hp):
        acc = acc_ref[h]
        on = acc[:DA_DV] / acc[DA_DV:DA_DV + 1]
        o = on[:, :tq] - lam * on[:, tq:]
        msq = jnp.mean(o * o, axis=0, keepdims=True)
        o = o * lax.rsqrt(msq + EPS) * sg_ref[...] * (1.0 - LAMBDA_INIT)
        o_ref[:, hcols(h)] = o.T.astype(BF16)


def _attention(proj3, lq1, lk1, lq2, lk2, subln_col, tq=256, hp=DA_HEADS):
    b, s, _ = proj3.shape
    ng = DA_HEADS // hp
    lam_spec = pl.BlockSpec((1, DA_DQK), lambda bi, g, i: (0, 0))
    return pl.pallas_call(
        functools.partial(_attn_kernel, tq=tq, hp=hp),
        grid=(b, ng, s // tq),
        in_specs=[
            pl.BlockSpec((None, tq, hp * LANES), lambda bi, g, i: (bi, i, g)),
            pl.BlockSpec((None, s, hp * LANES), lambda bi, g, i: (bi, 0, ng + g), pipeline_mode=pl.Buffered(1)),
            pl.BlockSpec((None, s, hp * LANES), lambda bi, g, i: (bi, 0, 2 * ng + g), pipeline_mode=pl.Buffered(1)),
            lam_spec, lam_spec, lam_spec, lam_spec,
            pl.BlockSpec((DA_DV, 1), lambda bi, g, i: (0, 0)),
        ],
        out_specs=pl.BlockSpec((None, tq, hp * DA_DV), lambda bi, g, i: (bi, i, g)),
        out_shape=jax.ShapeDtypeStruct((b, s, DA_HEADS * DA_DV), BF16),
        scratch_shapes=[pltpu.VMEM((hp, 2 * tq, LANES), BF16),
                        pltpu.VMEM((hp, s // tq, DA_DV + ONES_ROWS, tq), BF16),
                        pltpu.VMEM((hp, DA_DV + ONES_ROWS, 2 * tq), F32),
                        pltpu.VMEM((hp, tq, 2 * tq), BF16)],
        compiler_params=_params("parallel", "parallel", "arbitrary"),
        name="diff_attn",
    )(proj3, proj3, proj3, lq1, lk1, lq2, lk2, subln_col)


def _mix_kernel(su_ref, sv_ref, ga_ref, gb_ref, ya_ref, x_ref, sgw_ref, sgb_ref, wa_ref, wb_ref,
                wo_ref, g2_ref, wq_ref, h_ref, xn_ref, qp_ref, yb_ref, *, tm):
    r = lax.broadcasted_iota(I32, (SG_BLOCK, SG_BLOCK), 0)
    c = lax.broadcasted_iota(I32, (SG_BLOCK, SG_BLOCK), 1)
    causal = (r // CHUNK) >= (c // CHUNK)
    for g in range(SG_GROUPS):
        wm = jnp.where(causal, sgw_ref[g], 0.0).astype(BF16)
        cols = slice(g * SG_BLOCK, (g + 1) * SG_BLOCK)
        for n in range(tm // SG_BLOCK):
            rows = slice(n * SG_BLOCK, (n + 1) * SG_BLOCK)
            mixed = jnp.dot(wm, sv_ref[rows, cols], preferred_element_type=F32) + sgb_ref[g]
            yb_ref[rows, cols] = (su_ref[rows, cols].astype(F32) * mixed).astype(BF16)
    a = jnp.dot(ya_ref[...], wa_ref[...], preferred_element_type=F32)
    bm = jnp.dot(yb_ref[...], wb_ref[...], preferred_element_type=F32)
    merged = ga_ref[...].astype(F32) * a + gb_ref[...].astype(F32) * bm
    h = x_ref[...] + jnp.dot(merged.astype(BF16), wo_ref[...], preferred_element_type=F32)
    h_ref[...] = h
    ms = jnp.mean(h * h, axis=-1, keepdims=True)
    xn = h * lax.rsqrt(ms + EPS) * g2_ref[...]
    xn_ref[...] = xn
    qp_ref[...] = jnp.dot(xn.astype(BF16), wq_ref[...], preferred_element_type=F32)


def _mix(proj, ya, x2, sgw, sgb, wa, wb, wo, g2, wq, tm=256):
    t = x2.shape[0]
    nq = wq.shape[1]
    full = lambda shape: pl.BlockSpec(shape, lambda i: (0,) * len(shape))
    col = lambda cb: pl.BlockSpec((tm, D_MODEL), lambda i: (i, cb))
    return pl.pallas_call(
        functools.partial(_mix_kernel, tm=tm),
        grid=(t // tm,),
        in_specs=[col(3), col(4), col(5), col(6), col(0), col(0),
                  full((SG_GROUPS, SG_BLOCK, SG_BLOCK)), full((SG_GROUPS, SG_BLOCK, 1)),
                  full((D_MODEL, D_MODEL)), full((D_MODEL, D_MODEL)), full((D_MODEL, D_MODEL)),
                  full((1, D_MODEL)), full((D_MODEL, nq))],
        out_specs=[col(0), col(0), pl.BlockSpec((tm, nq), lambda i: (i, 0))],
        out_shape=[jax.ShapeDtypeStruct((t, D_MODEL), F32), jax.ShapeDtypeStruct((t, D_MODEL), F32),
                   jax.ShapeDtypeStruct((t, nq), F32)],
        scratch_shapes=[pltpu.VMEM((tm, D_MODEL), BF16)],
        compiler_params=_params("parallel"),
        name="gmlp_merge",
    )(proj, proj, proj, proj, ya, x2, sgw, sgb, wa, wb, wo, g2, wq)


_GRID_GROUPS = ((0, 0, 8), (0, 8, 8)) + tuple((a, 0, min(8, PEER_TOPK // (a + 1))) for a in range(1, 8))


def _topk_kernel(qp_ref, sk_ref, idx_ref, gate_ref, sv_ref, si_ref, ts_ref, te_ref, *, tt):
    neg = -jnp.inf
    iota_k = lax.broadcasted_iota(I32, (N_KEYS, tt), 0).astype(F32)
    ncand = PEER_TOPK * PEER_TOPK
    sub8 = lax.broadcasted_iota(I32, (8, tt), 0).astype(F32)
    for h in range(PEER_HEADS):
        for p in range(2):
            c0 = (h * 2 + p) * D_KEY_HALF
            qh = qp_ref[:, c0:c0 + D_KEY_HALF].astype(BF16)
            s = lax.dot_general(sk_ref[p].astype(BF16), qh, (((1,), (1,)), ((), ())),
                                preferred_element_type=F32)
            for k in range(PEER_TOPK):
                m = jnp.max(s, axis=0, keepdims=True)
                ii = jnp.min(jnp.where(s == m, iota_k, float(N_KEYS)), axis=0, keepdims=True)
                s = jnp.where(iota_k == ii, neg, s)
                sv_ref[p, k:k + 1, :] = m
                si_ref[p, k:k + 1, :] = ii
        cands, cposs, cidxs = [], [], []
        for a, b0, nvalid in _GRID_GROUPS:
            c = sv_ref[0, a:a + 1, :] + sv_ref[1, b0:b0 + 8, :]
            cands.append(c if nvalid == 8 else jnp.where(sub8 < float(nvalid), c, neg))
            cposs.append(sub8 + float(a * PEER_TOPK + b0))
            cidxs.append(si_ref[0, a:a + 1, :] * float(N_KEYS) + si_ref[1, b0:b0 + 8, :])
        cands.append(sv_ref[0, 8:16, :] + sv_ref[1, 0:1, :])
        cposs.append((sub8 + 8.0) * float(PEER_TOPK))
        cidxs.append(si_ref[0, 8:16, :] * float(N_KEYS) + si_ref[1, 0:1, :])
        cand = jnp.concatenate(cands, axis=0)
        cpos = jnp.concatenate(cposs, axis=0)
        cidx = jnp.concatenate(cidxs, axis=0)
        for k in range(PEER_TOPK):
            m = jnp.max(cand, axis=0, keepdims=True)
            pos = jnp.min(jnp.where(cand == m, cpos, float(ncand)), axis=0, keepdims=True)
            sel = cpos == pos
            e = jnp.max(jnp.where(sel, cidx, -1.0), axis=0, keepdims=True)
            cand = jnp.where(sel, neg, cand)
            ts_ref[h * PEER_TOPK + k:h * PEER_TOPK + k + 1, :] = m
            te_ref[h * PEER_TOPK + k:h * PEER_TOPK + k + 1, :] = e
        rows = slice(h * PEER_TOPK, (h + 1) * PEER_TOPK)
        ts = ts_ref[rows, :]
        ex = jnp.exp(ts - jnp.max(ts, axis=0, keepdims=True))
        gate_ref[rows, :] = ex / jnp.sum(ex, axis=0, keepdims=True)
    idx_ref[...] = (te_ref[...] * float(ROW_SUBLANES)).T.astype(I32)


def _topk(qp, subkeys, tt=LANES):
    t = qp.shape[0]
    nblk = t // tt
    out_spec = pl.BlockSpec((None, N_SEL, tt), lambda i: (i, 0, 0))
    return pl.pallas_call(
        functools.partial(_topk_kernel, tt=tt),
        grid=(nblk,),
        in_specs=[pl.BlockSpec((tt, qp.shape[1]), lambda i: (i, 0)),
                  pl.BlockSpec((2, N_KEYS, D_KEY_HALF), lambda i: (0, 0, 0))],
        out_specs=[pl.BlockSpec((None, tt, N_SEL), lambda i: (i, 0, 0)), out_spec],
        out_shape=[jax.ShapeDtypeStruct((nblk, tt, N_SEL), I32),
                   jax.ShapeDtypeStruct((nblk, N_SEL, tt), F32)],
        scratch_shapes=[pltpu.VMEM((2, PEER_TOPK, tt), F32), pltpu.VMEM((2, PEER_TOPK, tt), F32),
                        pltpu.VMEM((N_SEL, tt), F32), pltpu.VMEM((N_SEL, tt), F32)],
        compiler_params=_params("parallel"),
        name="peer_topk",
    )(qp, subkeys)


def _gather_rows(idx_ref, tab_ref, tile_ref, t):
    for j in range(N_SEL):
        start = pl.multiple_of(idx_ref[t, j], ROW_SUBLANES)
        tile_ref[pl.ds(j, ROW_SUBLANES, stride=TILE_STRIDE), :] = tab_ref[pl.ds(start, ROW_SUBLANES), :]


def _unpack(tile_ref, s):
    g = tile_ref[s * TILE_STRIDE:s * TILE_STRIDE + N_SEL, :]
    lo = pltpu.unpack_elementwise(g, index=0, packed_dtype=BF16, unpacked_dtype=F32)
    hi = pltpu.unpack_elementwise(g, index=1, packed_dtype=BF16, unpacked_dtype=F32)
    return lo, hi


def _next_token(t, tb):
    return jnp.minimum(t, tb - 1)


def _peer_u_kernel(idx_ref, x_ref, gate_ref, tab_ref, w_ref, tile_a, tile_b, hid_ref, part_ref, *, tb):
    lane = lax.broadcasted_iota(I32, (N_SEL, tb), 1)

    def products(tile_ref, t, slot):
        part = jnp.zeros((N_SEL, LANES), F32)
        for s in range(ROW_SUBLANES):
            lo, hi = _unpack(tile_ref, s)
            part = part + lo * x_ref[pl.ds(t, 1), s, :] + hi * x_ref[pl.ds(t, 1), ROW_SUBLANES + s, :]
        part_ref[slot] = part

    def insert(slot, t):
        col = jnp.sum(part_ref[slot], axis=1, keepdims=True)
        hid_ref[...] = jnp.where(lane == t, col, hid_ref[...])

    part_ref[...] = jnp.zeros(part_ref.shape, F32)
    hid_ref[...] = jnp.zeros(hid_ref.shape, F32)
    _gather_rows(idx_ref, tab_ref, tile_a, 0)

    def body(p, carry):
        t = 2 * p
        insert(0, t - 2)
        insert(1, t - 1)
        _gather_rows(idx_ref, tab_ref, tile_b, t + 1)
        products(tile_a, t, 0)
        _gather_rows(idx_ref, tab_ref, tile_a, _next_token(t + 2, tb))
        products(tile_b, t + 1, 1)
        return carry

    lax.fori_loop(0, tb // 2, body, 0)
    insert(0, tb - 2)
    insert(1, tb - 1)
    w_ref[...] = gate_ref[...] * _gelu(hid_ref[...])


def _peer_v_kernel(idx_ref, w_ref, tab_ref, y_ref, tile_a, tile_b, *, tb):
    lane = lax.broadcasted_iota(I32, (N_SEL, tb), 1)

    def weighted_sum(tile_ref, t):
        wcol = jnp.sum(jnp.where(lane == t, w_ref[...], 0.0), axis=1, keepdims=True)
        for s in range(ROW_SUBLANES):
            lo, hi = _unpack(tile_ref, s)
            y_ref[pl.ds(t, 1), s, :] = jnp.sum(lo * wcol, axis=0, keepdims=True)
            y_ref[pl.ds(t, 1), ROW_SUBLANES + s, :] = jnp.sum(hi * wcol, axis=0, keepdims=True)

    _gather_rows(idx_ref, tab_ref, tile_a, 0)

    def body(p, carry):
        t = 2 * p
        _gather_rows(idx_ref, tab_ref, tile_b, t + 1)
        weighted_sum(tile_a, t)
        _gather_rows(idx_ref, tab_ref, tile_a, _next_token(t + 2, tb))
        weighted_sum(tile_b, t + 1)
        return carry

    lax.fori_loop(0, tb // 2, body, 0)


def _peer_specs(tb):
    idx_spec = pl.BlockSpec((None, tb, N_SEL), lambda i: (i, 0, 0), memory_space=pltpu.SMEM)
    blk_spec = pl.BlockSpec((None, N_SEL, tb), lambda i: (i, 0, 0))
    tok_spec = pl.BlockSpec((tb, 2 * ROW_SUBLANES, LANES), lambda i: (i, 0, 0))
    tab_spec = pl.BlockSpec(memory_space=pltpu.VMEM)
    tile = pltpu.VMEM((ROW_SUBLANES * TILE_STRIDE, LANES), U32)
    return idx_spec, blk_spec, tok_spec, tab_spec, [tile, tile]


def _peer_u(idx, xn3, gate, tab, tb=LANES):
    nblk = idx.shape[0]
    idx_spec, blk_spec, tok_spec, tab_spec, tiles = _peer_specs(tb)
    return pl.pallas_call(
        functools.partial(_peer_u_kernel, tb=tb),
        grid=(nblk,),
        in_specs=[idx_spec, tok_spec, blk_spec, tab_spec],
        out_specs=blk_spec,
        out_shape=jax.ShapeDtypeStruct((nblk, N_SEL, tb), F32),
        scratch_shapes=tiles + [pltpu.VMEM((N_SEL, tb), F32), pltpu.VMEM((2, N_SEL, LANES), F32)],
        compiler_params=_params("arbitrary"),
        name="peer_u",
    )(idx, xn3, gate, tab)


def _peer_v(idx, w, tab, tb=LANES):
    nblk = idx.shape[0]
    idx_spec, blk_spec, tok_spec, tab_spec, tiles = _peer_specs(tb)
    return pl.pallas_call(
        functools.partial(_peer_v_kernel, tb=tb),
        grid=(nblk,),
        in_specs=[idx_spec, blk_spec, tab_spec],
        out_specs=tok_spec,
        out_shape=jax.ShapeDtypeStruct((nblk * tb, 2 * ROW_SUBLANES, LANES), F32),
        scratch_shapes=tiles,
        compiler_params=_params("arbitrary"),
        name="peer_v",
    )(idx, w, tab)


def _final_kernel(h_ref, y_ref, g_ref, o_ref):
    h = h_ref[...] + y_ref[...]
    ms = jnp.mean(h * h, axis=-1, keepdims=True)
    o_ref[...] = h * lax.rsqrt(ms + EPS) * g_ref[...]


def _final(h, y, g, tm=1024):
    t = h.shape[0]
    tm = min(tm, t)
    spec = pl.BlockSpec((tm, D_MODEL), lambda i: (i, 0))
    return pl.pallas_call(
        _final_kernel,
        grid=(t // tm,),
        in_specs=[spec, spec, pl.BlockSpec((1, D_MODEL), lambda i: (0, 0))],
        out_specs=spec,
        out_shape=jax.ShapeDtypeStruct((t, D_MODEL), F32),
        compiler_params=_params("parallel"),
        name="final_norm",
    )(h, y, g)


def _head_major_columns():
    h = jnp.arange(DA_HEADS)[:, None, None]
    m = jnp.arange(2)[None, :, None]
    d = jnp.arange(DA_DQK)[None, None, :]
    return (m * (DA_HEADS * DA_DQK) + h * DA_DQK + d).reshape(-1)


def kernel(x, norm1_g, w_in, lambda_q1, lambda_k1, lambda_q2, lambda_k2, da_subln_g, sg_ln_g, sg_ln_b,
           sg_w, sg_b, w_branch_attn, w_branch_sg, w_out, norm2_g, peer_w_query, peer_subkeys, peer_u,
           peer_v, final_g):
    b, s, d = x.shape
    t = b * s
    assert w_in.shape[0] == 1, "single-layer block"
    x2 = x.reshape(t, d)
    row = lambda a: a.reshape(1, -1)

    perm = _head_major_columns()
    qk = 2 * DA_HEADS * DA_DQK
    w = w_in[0]
    w = jnp.concatenate([w[:, :qk][:, perm], w[:, qk:2 * qk][:, perm], w[:, 2 * qk:]], axis=1).astype(BF16)

    proj = _inproj(x2, row(norm1_g[0]), w, row(sg_ln_g[0]), row(sg_ln_b[0]))
    ya = _attention(proj.reshape(b, s, -1), row(lambda_q1[0]), row(lambda_k1[0]), row(lambda_q2[0]),
                    row(lambda_k2[0]), da_subln_g[0].reshape(-1, 1))
    h, xn, qp = _mix(proj, ya.reshape(t, -1), x2, sg_w[0], sg_b[0][:, :, None],
                     w_branch_attn[0].astype(BF16), w_branch_sg[0].astype(BF16), w_out[0].astype(BF16),
                     row(norm2_g[0]), peer_w_query[0].astype(BF16))
    idx, gate = _topk(qp, peer_subkeys[0])
    wts = _peer_u(idx, xn.reshape(t, 2 * ROW_SUBLANES, LANES), gate, _pack_table(peer_u[0]))
    y = _peer_v(idx, wts, _pack_table(peer_v[0]))
    out = _final(h, y.reshape(t, d), row(final_g))
    return out.reshape(b, s, d)
```

```python
import functools
import math

import jax
import jax.numpy as jnp
from jax import lax
from jax.experimental import pallas as pl
from jax.experimental.pallas import tpu as pltpu

F32 = jnp.float32
BF16 = jnp.bfloat16
I32 = jnp.int32
U32 = jnp.uint32

D_MODEL = 1024
CHUNK = 64
EPS = 1e-6
DA_HEADS = 8
DA_DQK = 64
DA_DV = 128
SG_BLOCK = 128
SG_GROUPS = 8
PEER_HEADS = 8
N_KEYS = 128
PEER_TOPK = 16
D_KEY_HALF = 128
N_SEL = PEER_HEADS * PEER_TOPK
LANES = 128
WORDS_PER_ROW = D_MODEL // 2
ROW_SUBLANES = WORDS_PER_ROW // LANES
TILE_STRIDE = N_SEL + 8
VMEM_LIMIT = 56 * 1024 * 1024
LAMBDA_INIT = 0.8 - 0.6 * math.exp(-0.3 * 0)
SQRT_HALF = 0.7071067811865476
QK_SCALE = DA_DQK ** -0.5 * math.log2(math.e)
ONES_ROWS = 16


def _gelu(x):
    return 0.5 * x * (1.0 + lax.erf(x * SQRT_HALF))


def _params(*sem):
    return pltpu.CompilerParams(dimension_semantics=sem, vmem_limit_bytes=VMEM_LIMIT)


def _pack_kernel(t_ref, o_ref):
    t = t_ref[...]
    words = pltpu.pack_elementwise([t[:, :WORDS_PER_ROW], t[:, WORDS_PER_ROW:]], packed_dtype=BF16)
    for s in range(ROW_SUBLANES):
        o_ref[pl.ds(s, t.shape[0], stride=ROW_SUBLANES), :] = words[:, s * LANES:(s + 1) * LANES]


def _pack_table(tab, rows=512):
    n = tab.shape[0]
    return pl.pallas_call(
        _pack_kernel,
        grid=(n // rows,),
        in_specs=[pl.BlockSpec((rows, D_MODEL), lambda i: (i, 0))],
        out_specs=pl.BlockSpec((rows * ROW_SUBLANES, LANES), lambda i: (i, 0)),
        out_shape=jax.ShapeDtypeStruct((n * ROW_SUBLANES, LANES), U32),
        compiler_params=_params("parallel"),
        name="pack_table",
    )(tab)


def _inproj_kernel(x_ref, g_ref, w_ref, lng_ref, lnb_ref, o_ref):
    x = x_ref[...]
    ms = jnp.mean(x * x, axis=-1, keepdims=True)
    xn = (x * lax.rsqrt(ms + EPS) * g_ref[...]).astype(BF16)

    def block(j):
        cols = slice(j * D_MODEL, (j + 1) * D_MODEL)
        return cols, jnp.dot(xn, w_ref[:, cols], preferred_element_type=F32)

    cols, acc = block(0)
    o_ref[:, cols] = (acc * QK_SCALE).astype(BF16)
    for j in (1, 2):
        cols, acc = block(j)
        o_ref[:, cols] = acc.astype(BF16)
    cols, acc = block(3)
    o_ref[:, cols] = _gelu(acc).astype(BF16)
    cols, acc = block(4)
    v = _gelu(acc)
    mu = jnp.mean(v, axis=-1, keepdims=True)
    var = jnp.mean(jnp.square(v - mu), axis=-1, keepdims=True)
    o_ref[:, cols] = ((v - mu) * lax.rsqrt(var + EPS) * lng_ref[...] + lnb_ref[...]).astype(BF16)
    for j in (5, 6):
        cols, acc = block(j)
        o_ref[:, cols] = (1.0 / (1.0 + jnp.exp(-acc))).astype(BF16)


def _inproj(x2, g1, w, lng, lnb, tm=256):
    t = x2.shape[0]
    assert w.shape[1] == 7 * D_MODEL
    vec = pl.BlockSpec((1, D_MODEL), lambda i: (0, 0))
    return pl.pallas_call(
        _inproj_kernel,
        grid=(t // tm,),
        in_specs=[pl.BlockSpec((tm, D_MODEL), lambda i: (i, 0)), vec,
                  pl.BlockSpec(w.shape, lambda i: (0, 0), pipeline_mode=pl.Buffered(1)), vec, vec],
        out_specs=pl.BlockSpec((tm, w.shape[1]), lambda i: (i, 0)),
        out_shape=jax.ShapeDtypeStruct((t, w.shape[1]), BF16),
        compiler_params=_params("parallel"),
        name="inproj",
    )(x2, g1, w, lng, lnb)


def _attn_kernel(q_ref, k_ref, v_ref, lq1_ref, lk1_ref, lq2_ref, lk2_ref, sg_ref, o_ref,
                 qbd_ref, vt_ref, acc_ref, pt_ref, *, tq, hp):
    i = pl.program_id(2)
    nkv = vt_ref.shape[1]
    hcols = lambda h: slice(h * LANES, (h + 1) * LANES)

    @pl.when(i == 0)
    def _():
        for h in range(hp):
            for j in range(nkv):
                vt_ref[h, j, :DA_DV, :] = v_ref[j * tq:(j + 1) * tq, hcols(h)].T
                vt_ref[h, j, DA_DV:, :] = jnp.ones((ONES_ROWS, tq), BF16)

    lane = lax.broadcasted_iota(I32, (tq, LANES), 1)
    for h in range(hp):
        q = q_ref[:, hcols(h)]
        zero = jnp.zeros_like(q)
        qbd_ref[h] = jnp.concatenate([jnp.where(lane < DA_DQK, q, zero), jnp.where(lane >= DA_DQK, q, zero)],
                                     axis=0)
        acc_ref[h] = jnp.zeros(acc_ref.shape[1:], F32)
        pt_ref[h] = jnp.zeros(pt_ref.shape[1:], BF16)

    def value_update(j, alphas):
        pvs = [jnp.dot(vt_ref[h, j], pt_ref[h], preferred_element_type=F32) for h in range(hp)]
        for h in range(hp):
            acc_ref[h] = alphas[h] * acc_ref[h] + pvs[h]

    def step(j, carry, mask):
        ms, alphas = carry
        kstart = pl.multiple_of(j * tq, tq)
        jp = jnp.maximum(j - 1, 0)
        sts, pvs = [], []
        for h in range(hp):
            sts.append(lax.dot_general(k_ref[pl.ds(kstart, tq), hcols(h)], qbd_ref[h], (((1,), (1,)), ((), ())),
                                       preferred_element_type=F32))
            pvs.append(jnp.dot(vt_ref[h, jp], pt_ref[h], preferred_element_type=F32))
        new_ms, new_alphas = [], []
        for h in range(hp):
            st = sts[h] if mask is None else jnp.where(mask, sts[h], -1e30)
            m_new = jnp.maximum(ms[h], jnp.max(st, axis=0, keepdims=True))
            new_alphas.append(jnp.exp2(ms[h] - m_new))
            new_ms.append(m_new)
            acc_ref[h] = alphas[h] * acc_ref[h] + pvs[h]
            pt_ref[h] = jnp.exp2(st - m_new).astype(BF16)
        return tuple(new_ms), tuple(new_alphas)

    init = (tuple(jnp.full((1, 2 * tq), -jnp.inf, F32) for _ in range(hp)),
            tuple(jnp.ones((1, 2 * tq), F32) for _ in range(hp)))
    carry = lax.fori_loop(0, i, lambda j, c: step(j, c, None), init)
    key = lax.broadcasted_iota(I32, (tq, 2 * tq), 0)
    col = lax.broadcasted_iota(I32, (tq, 2 * tq), 1)
    qpos = jnp.where(col >= tq, col - tq, col)
    _, alphas = step(i, carry, (key // CHUNK) <= (qpos // CHUNK))
    value_update(i, alphas)

    lam = (jnp.exp(jnp.sum(lq1_ref[...] * lk1_ref[...], axis=-1, keepdims=True))
           - jnp.exp(jnp.sum(lq2_ref[...] * lk2_ref[...], axis=-1, keepdims=True)) + LAMBDA_INIT)
    for h in range(hp):
        acc = acc_ref[h]
        on = acc[:DA_DV] / acc[DA_DV:DA_DV + 1]
        o = on[:, :tq] - lam * on[:, tq:]
        msq = jnp.mean(o * o, axis=0, keepdims=True)
        o = o * lax.rsqrt(msq + EPS) * sg_ref[...] * (1.0 - LAMBDA_INIT)
        o_ref[:, hcols(h)] = o.T.astype(BF16)


def _attention(proj3, lq1, lk1, lq2, lk2, subln_col, tq=256, hp=DA_HEADS):
    b, s, _ = proj3.shape
    ng = DA_HEADS // hp
    lam_spec = pl.BlockSpec((1, DA_DQK), lambda bi, g, i: (0, 0))
    return pl.pallas_call(
        functools.partial(_attn_kernel, tq=tq, hp=hp),
        grid=(b, ng, s // tq),
        in_specs=[
            pl.BlockSpec((None, tq, hp * LANES), lambda bi, g, i: (bi, i, g)),
            pl.BlockSpec((None, s, hp * LANES), lambda bi, g, i: (bi, 0, ng + g), pipeline_mode=pl.Buffered(1)),
            pl.BlockSpec((None, s, hp * LANES), lambda bi, g, i: (bi, 0, 2 * ng + g), pipeline_mode=pl.Buffered(1)),
            lam_spec, lam_spec, lam_spec, lam_spec,
            pl.BlockSpec((DA_DV, 1), lambda bi, g, i: (0, 0)),
        ],
        out_specs=pl.BlockSpec((None, tq, hp * DA_DV), lambda bi, g, i: (bi, i, g)),
        out_shape=jax.ShapeDtypeStruct((b, s, DA_HEADS * DA_DV), BF16),
        scratch_shapes=[pltpu.VMEM((hp, 2 * tq, LANES), BF16),
                        pltpu.VMEM((hp, s // tq, DA_DV + ONES_ROWS, tq), BF16),
                        pltpu.VMEM((hp, DA_DV + ONES_ROWS, 2 * tq), F32),
                        pltpu.VMEM((hp, tq, 2 * tq), BF16)],
        compiler_params=_params("parallel", "parallel", "arbitrary"),
        name="diff_attn",
    )(proj3, proj3, proj3, lq1, lk1, lq2, lk2, subln_col)


def _mix_kernel(su_ref, sv_ref, ga_ref, gb_ref, ya_ref, x_ref, sgw_ref, sgb_ref, wa_ref, wb_ref,
                wo_ref, g2_ref, wq_ref, h_ref, xn_ref, qp_ref, yb_ref, *, tm):
    r = lax.broadcasted_iota(I32, (SG_BLOCK, SG_BLOCK), 0)
    c = lax.broadcasted_iota(I32, (SG_BLOCK, SG_BLOCK), 1)
    causal = (r // CHUNK) >= (c // CHUNK)
    for g in range(SG_GROUPS):
        wm = jnp.where(causal, sgw_ref[g], 0.0).astype(BF16)
        cols = slice(g * SG_BLOCK, (g + 1) * SG_BLOCK)
        for n in range(tm // SG_BLOCK):
            rows = slice(n * SG_BLOCK, (n + 1) * SG_BLOCK)
            mixed = jnp.dot(wm, sv_ref[rows, cols], preferred_element_type=F32) + sgb_ref[g]
            yb_ref[rows, cols] = (su_ref[rows, cols].astype(F32) * mixed).astype(BF16)
    a = jnp.dot(ya_ref[...], wa_ref[...], preferred_element_type=F32)
    bm = jnp.dot(yb_ref[...], wb_ref[...], preferred_element_type=F32)
    merged = ga_ref[...].astype(F32) * a + gb_ref[...].astype(F32) * bm
    h = x_ref[...] + jnp.dot(merged.astype(BF16), wo_ref[...], preferred_element_type=F32)
    h_ref[...] = h
    ms = jnp.mean(h * h, axis=-1, keepdims=True)
    xn = h * lax.rsqrt(ms + EPS) * g2_ref[...]
    xn_ref[...] = xn
    qp_ref[...] = jnp.dot(xn.astype(BF16), wq_ref[...], preferred_element_type=F32)


def _mix(proj, ya, x2, sgw, sgb, wa, wb, wo, g2, wq, tm=256):
    t = x2.shape[0]
    nq = wq.shape[1]
    full = lambda shape: pl.BlockSpec(shape, lambda i: (0,) * len(shape))
    col = lambda cb: pl.BlockSpec((tm, D_MODEL), lambda i: (i, cb))
    return pl.pallas_call(
        functools.partial(_mix_kernel, tm=tm),
        grid=(t // tm,),
        in_specs=[col(3), col(4), col(5), col(6), col(0), col(0),
                  full((SG_GROUPS, SG_BLOCK, SG_BLOCK)), full((SG_GROUPS, SG_BLOCK, 1)),
                  full((D_MODEL, D_MODEL)), full((D_MODEL, D_MODEL)), full((D_MODEL, D_MODEL)),
                  full((1, D_MODEL)), full((D_MODEL, nq))],
        out_specs=[col(0), col(0), pl.BlockSpec((tm, nq), lambda i: (i, 0))],
        out_shape=[jax.ShapeDtypeStruct((t, D_MODEL), F32), jax.ShapeDtypeStruct((t, D_MODEL), F32),
                   jax.ShapeDtypeStruct((t, nq), F32)],
        scratch_shapes=[pltpu.VMEM((tm, D_MODEL), BF16)],
        compiler_params=_params("parallel"),
        name="gmlp_merge",
    )(proj, proj, proj, proj, ya, x2, sgw, sgb, wa, wb, wo, g2, wq)


TOPK_LEVELS = 4
TOPK_LEVEL_ROWS = N_KEYS // TOPK_LEVELS
_GRID_GROUPS = ((0, 0, 8), (0, 8, 8)) + tuple((a, 0, min(8, PEER_TOPK // (a + 1))) for a in range(1, 8))


def _topk_kernel(qp_ref, sk_ref, idx_ref, gate_ref, sv_ref, si_ref, ts_ref, te_ref, *, tt):
    neg = -jnp.inf
    iota_l = lax.broadcasted_iota(I32, (TOPK_LEVEL_ROWS, tt), 0).astype(F32)
    ncand = PEER_TOPK * PEER_TOPK
    sub8 = lax.broadcasted_iota(I32, (8, tt), 0).astype(F32)
    for h in range(PEER_HEADS):
        for p in range(2):
            c0 = (h * 2 + p) * D_KEY_HALF
            qh = qp_ref[:, c0:c0 + D_KEY_HALF].astype(BF16)
            s = lax.dot_general(sk_ref[p].astype(BF16), qh, (((1,), (1,)), ((), ())),
                                preferred_element_type=F32)
            vals = [s[d * TOPK_LEVEL_ROWS:(d + 1) * TOPK_LEVEL_ROWS, :] for d in range(TOPK_LEVELS)]
            keys = [iota_l + float(d * TOPK_LEVEL_ROWS) for d in range(TOPK_LEVELS)]
            for hi in range(TOPK_LEVELS - 1, 0, -1):
                for d in range(hi):
                    swap = vals[d + 1] > vals[d]
                    vals[d], vals[d + 1] = (jnp.where(swap, vals[d + 1], vals[d]),
                                            jnp.where(swap, vals[d], vals[d + 1]))
                    keys[d], keys[d + 1] = (jnp.where(swap, keys[d + 1], keys[d]),
                                            jnp.where(swap, keys[d], keys[d + 1]))
            for k in range(PEER_TOPK):
                m = jnp.max(vals[0], axis=0, keepdims=True)
                ii = jnp.min(jnp.where(vals[0] == m, keys[0], float(N_KEYS)), axis=0, keepdims=True)
                pop = keys[0] == ii
                for d in range(TOPK_LEVELS - 1):
                    vals[d] = jnp.where(pop, vals[d + 1], vals[d])
                    keys[d] = jnp.where(pop, keys[d + 1], keys[d])
                vals[-1] = jnp.where(pop, neg, vals[-1])
                sv_ref[p, k:k + 1, :] = m
                si_ref[p, k:k + 1, :] = ii
        cands, cposs, cidxs = [], [], []
        for a, b0, nvalid in _GRID_GROUPS:
            c = sv_ref[0, a:a + 1, :] + sv_ref[1, b0:b0 + 8, :]
            cands.append(c if nvalid == 8 else jnp.where(sub8 < float(nvalid), c, neg))
            cposs.append(sub8 + float(a * PEER_TOPK + b0))
            cidxs.append(si_ref[0, a:a + 1, :] * float(N_KEYS) + si_ref[1, b0:b0 + 8, :])
        cands.append(sv_ref[0, 8:16, :] + sv_ref[1, 0:1, :])
        cposs.append((sub8 + 8.0) * float(PEER_TOPK))
        cidxs.append(si_ref[0, 8:16, :] * float(N_KEYS) + si_ref[1, 0:1, :])
        cand = jnp.concatenate(cands, axis=0)
        cpos = jnp.concatenate(cposs, axis=0)
        cidx = jnp.concatenate(cidxs, axis=0)
        for k in range(PEER_TOPK):
            m = jnp.max(cand, axis=0, keepdims=True)
            pos = jnp.min(jnp.where(cand == m, cpos, float(ncand)), axis=0, keepdims=True)
            sel = cpos == pos
            e = jnp.max(jnp.where(sel, cidx, -1.0), axis=0, keepdims=True)
            cand = jnp.where(sel, neg, cand)
            ts_ref[h * PEER_TOPK + k:h * PEER_TOPK + k + 1, :] = m
            te_ref[h * PEER_TOPK + k:h * PEER_TOPK + k + 1, :] = e
        rows = slice(h * PEER_TOPK, (h + 1) * PEER_TOPK)
        ts = ts_ref[rows, :]
        ex = jnp.exp(ts - jnp.max(ts, axis=0, keepdims=True))
        gate_ref[rows, :] = ex / jnp.sum(ex, axis=0, keepdims=True)
    idx_ref[...] = (te_ref[...] * float(ROW_SUBLANES)).T.astype(I32)


def _topk(qp, subkeys, tt=LANES):
    t = qp.shape[0]
    nblk = t // tt
    out_spec = pl.BlockSpec((None, N_SEL, tt), lambda i: (i, 0, 0))
    return pl.pallas_call(
        functools.partial(_topk_kernel, tt=tt),
        grid=(nblk,),
        in_specs=[pl.BlockSpec((tt, qp.shape[1]), lambda i: (i, 0)),
                  pl.BlockSpec((2, N_KEYS, D_KEY_HALF), lambda i: (0, 0, 0))],
        out_specs=[pl.BlockSpec((None, tt, N_SEL), lambda i: (i, 0, 0)), out_spec],
        out_shape=[jax.ShapeDtypeStruct((nblk, tt, N_SEL), I32),
                   jax.ShapeDtypeStruct((nblk, N_SEL, tt), F32)],
        scratch_shapes=[pltpu.VMEM((2, PEER_TOPK, tt), F32), pltpu.VMEM((2, PEER_TOPK, tt), F32),
                        pltpu.VMEM((N_SEL, tt), F32), pltpu.VMEM((N_SEL, tt), F32)],
        compiler_params=_params("parallel"),
        name="peer_topk",
    )(qp, subkeys)


def _gather_rows(idx_ref, tab_ref, tile_ref, t):
    for j in range(N_SEL):
        start = pl.multiple_of(idx_ref[t, j], ROW_SUBLANES)
        tile_ref[pl.ds(j, ROW_SUBLANES, stride=TILE_STRIDE), :] = tab_ref[pl.ds(start, ROW_SUBLANES), :]


def _unpack(tile_ref, s):
    g = tile_ref[s * TILE_STRIDE:s * TILE_STRIDE + N_SEL, :]
    lo = pltpu.unpack_elementwise(g, index=0, packed_dtype=BF16, unpacked_dtype=F32)
    hi = pltpu.unpack_elementwise(g, index=1, packed_dtype=BF16, unpacked_dtype=F32)
    return lo, hi


def _next_token(t, tb):
    return jnp.minimum(t, tb - 1)


def _peer_u_kernel(idx_ref, x_ref, gate_ref, tab_ref, w_ref, tile_a, tile_b, hid_ref, part_ref, *, tb):
    lane = lax.broadcasted_iota(I32, (N_SEL, tb), 1)

    def products(tile_ref, t, slot):
        part = jnp.zeros((N_SEL, LANES), F32)
        for s in range(ROW_SUBLANES):
            lo, hi = _unpack(tile_ref, s)
            part = part + lo * x_ref[pl.ds(t, 1), s, :] + hi * x_ref[pl.ds(t, 1), ROW_SUBLANES + s, :]
        part_ref[slot] = part

    def insert(slot, t):
        col = jnp.sum(part_ref[slot], axis=1, keepdims=True)
        hid_ref[...] = jnp.where(lane == t, col, hid_ref[...])

    part_ref[...] = jnp.zeros(part_ref.shape, F32)
    hid_ref[...] = jnp.zeros(hid_ref.shape, F32)
    _gather_rows(idx_ref, tab_ref, tile_a, 0)

    def body(p, carry):
        t = 2 * p
        insert(0, t - 2)
        insert(1, t - 1)
        _gather_rows(idx_ref, tab_ref, tile_b, t + 1)
        products(tile_a, t, 0)
        _gather_rows(idx_ref, tab_ref, tile_a, _next_token(t + 2, tb))
        products(tile_b, t + 1, 1)
        return carry

    lax.fori_loop(0, tb // 2, body, 0)
    insert(0, tb - 2)
    insert(1, tb - 1)
    w_ref[...] = gate_ref[...] * _gelu(hid_ref[...])


def _peer_v_kernel(idx_ref, w_ref, tab_ref, y_ref, tile_a, tile_b, *, tb):
    lane = lax.broadcasted_iota(I32, (N_SEL, tb), 1)

    def weighted_sum(tile_ref, t):
        wcol = jnp.sum(jnp.where(lane == t, w_ref[...], 0.0), axis=1, keepdims=True)
        for s in range(ROW_SUBLANES):
            lo, hi = _unpack(tile_ref, s)
            y_ref[pl.ds(t, 1), s, :] = jnp.sum(lo * wcol, axis=0, keepdims=True)
            y_ref[pl.ds(t, 1), ROW_SUBLANES + s, :] = jnp.sum(hi * wcol, axis=0, keepdims=True)

    _gather_rows(idx_ref, tab_ref, tile_a, 0)

    def body(p, carry):
        t = 2 * p
        _gather_rows(idx_ref, tab_ref, tile_b, t + 1)
        weighted_sum(tile_a, t)
        _gather_rows(idx_ref, tab_ref, tile_a, _next_token(t + 2, tb))
        weighted_sum(tile_b, t + 1)
        return carry

    lax.fori_loop(0, tb // 2, body, 0)


def _peer_specs(tb):
    idx_spec = pl.BlockSpec((None, tb, N_SEL), lambda i: (i, 0, 0), memory_space=pltpu.SMEM)
    blk_spec = pl.BlockSpec((None, N_SEL, tb), lambda i: (i, 0, 0))
    tok_spec = pl.BlockSpec((tb, 2 * ROW_SUBLANES, LANES), lambda i: (i, 0, 0))
    tab_spec = pl.BlockSpec(memory_space=pltpu.VMEM)
    tile = pltpu.VMEM((ROW_SUBLANES * TILE_STRIDE, LANES), U32)
    return idx_spec, blk_spec, tok_spec, tab_spec, [tile, tile]


def _peer_u(idx, xn3, gate, tab, tb=LANES):
    nblk = idx.shape[0]
    idx_spec, blk_spec, tok_spec, tab_spec, tiles = _peer_specs(tb)
    return pl.pallas_call(
        functools.partial(_peer_u_kernel, tb=tb),
        grid=(nblk,),
        in_specs=[idx_spec, tok_spec, blk_spec, tab_spec],
        out_specs=blk_spec,
        out_shape=jax.ShapeDtypeStruct((nblk, N_SEL, tb), F32),
        scratch_shapes=tiles + [pltpu.VMEM((N_SEL, tb), F32), pltpu.VMEM((2, N_SEL, LANES), F32)],
        compiler_params=_params("arbitrary"),
        name="peer_u",
    )(idx, xn3, gate, tab)


def _peer_v(idx, w, tab, tb=LANES):
    nblk = idx.shape[0]
    idx_spec, blk_spec, tok_spec, tab_spec, tiles = _peer_specs(tb)
    return pl.pallas_call(
        functools.partial(_peer_v_kernel, tb=tb),
        grid=(nblk,),
        in_specs=[idx_spec, blk_spec, tab_spec],
        out_specs=tok_spec,
        out_shape=jax.ShapeDtypeStruct((nblk * tb, 2 * ROW_SUBLANES, LANES), F32),
        scratch_shapes=tiles,
        compiler_params=_params("arbitrary"),
        name="peer_v",
    )(idx, w, tab)


def _final_kernel(h_ref, y_ref, g_ref, o_ref):
    h = h_ref[...] + y_ref[...]
    ms = jnp.mean(h * h, axis=-1, keepdims=True)
    o_ref[...] = h * lax.rsqrt(ms + EPS) * g_ref[...]


def _final(h, y, g, tm=1024):
    t = h.shape[0]
    tm = min(tm, t)
    spec = pl.BlockSpec((tm, D_MODEL), lambda i: (i, 0))
    return pl.pallas_call(
        _final_kernel,
        grid=(t // tm,),
        in_specs=[spec, spec, pl.BlockSpec((1, D_MODEL), lambda i: (0, 0))],
        out_specs=spec,
        out_shape=jax.ShapeDtypeStruct((t, D_MODEL), F32),
        compiler_params=_params("parallel"),
        name="final_norm",
    )(h, y, g)


def _head_major_columns():
    h = jnp.arange(DA_HEADS)[:, None, None]
    m = jnp.arange(2)[None, :, None]
    d = jnp.arange(DA_DQK)[None, None, :]
    return (m * (DA_HEADS * DA_DQK) + h * DA_DQK + d).reshape(-1)


def kernel(x, norm1_g, w_in, lambda_q1, lambda_k1, lambda_q2, lambda_k2, da_subln_g, sg_ln_g, sg_ln_b,
           sg_w, sg_b, w_branch_attn, w_branch_sg, w_out, norm2_g, peer_w_query, peer_subkeys, peer_u,
           peer_v, final_g):
    b, s, d = x.shape
    t = b * s
    assert w_in.shape[0] == 1, "single-layer block"
    x2 = x.reshape(t, d)
    row = lambda a: a.reshape(1, -1)

    perm = _head_major_columns()
    qk = 2 * DA_HEADS * DA_DQK
    w = w_in[0]
    w = jnp.concatenate([w[:, :qk][:, perm], w[:, qk:2 * qk][:, perm], w[:, 2 * qk:]], axis=1).astype(BF16)

    proj = _inproj(x2, row(norm1_g[0]), w, row(sg_ln_g[0]), row(sg_ln_b[0]))
    ya = _attention(proj.reshape(b, s, -1), row(lambda_q1[0]), row(lambda_k1[0]), row(lambda_q2[0]),
                    row(lambda_k2[0]), da_subln_g[0].reshape(-1, 1))
    h, xn, qp = _mix(proj, ya.reshape(t, -1), x2, sg_w[0], sg_b[0][:, :, None],
                     w_branch_attn[0].astype(BF16), w_branch_sg[0].astype(BF16), w_out[0].astype(BF16),
                     row(norm2_g[0]), peer_w_query[0].astype(BF16))
    idx, gate = _topk(qp, peer_subkeys[0])
    wts = _peer_u(idx, xn.reshape(t, 2 * ROW_SUBLANES, LANES), gate, _pack_table(peer_u[0]))
    y = _peer_v(idx, wts, _pack_table(peer_v[0]))
    out = _final(h, y.reshape(t, d), row(final_g))
    return out.reshape(b, s, d)
```

```python
import functools
import math

import jax
import jax.numpy as jnp
from jax import lax
from jax.experimental import pallas as pl
from jax.experimental.pallas import tpu as pltpu

F32 = jnp.float32
BF16 = jnp.bfloat16
I32 = jnp.int32
U32 = jnp.uint32

D_MODEL = 1024
CHUNK = 64
EPS = 1e-6
DA_HEADS = 8
DA_DQK = 64
DA_DV = 128
SG_BLOCK = 128
SG_GROUPS = 8
PEER_HEADS = 8
N_KEYS = 128
PEER_TOPK = 16
D_KEY_HALF = 128
N_SEL = PEER_HEADS * PEER_TOPK
LANES = 128
WORDS_PER_ROW = D_MODEL // 2
ROW_SUBLANES = WORDS_PER_ROW // LANES
TILE_STRIDE = N_SEL + 8
VMEM_LIMIT = 56 * 1024 * 1024
LAMBDA_INIT = 0.8 - 0.6 * math.exp(-0.3 * 0)
SQRT_HALF = 0.7071067811865476
QK_SCALE = DA_DQK ** -0.5 * math.log2(math.e)
ONES_ROWS = 16


def _gelu(x):
    return 0.5 * x * (1.0 + lax.erf(x * SQRT_HALF))


def _params(*sem):
    return pltpu.CompilerParams(dimension_semantics=sem, vmem_limit_bytes=VMEM_LIMIT)


def _pack_kernel(t_ref, o_ref):
    t = t_ref[...]
    words = pltpu.pack_elementwise([t[:, :WORDS_PER_ROW], t[:, WORDS_PER_ROW:]], packed_dtype=BF16)
    for s in range(ROW_SUBLANES):
        o_ref[pl.ds(s, t.shape[0], stride=ROW_SUBLANES), :] = words[:, s * LANES:(s + 1) * LANES]


def _pack_table(tab, rows=512):
    n = tab.shape[0]
    return pl.pallas_call(
        _pack_kernel,
        grid=(n // rows,),
        in_specs=[pl.BlockSpec((rows, D_MODEL), lambda i: (i, 0))],
        out_specs=pl.BlockSpec((rows * ROW_SUBLANES, LANES), lambda i: (i, 0)),
        out_shape=jax.ShapeDtypeStruct((n * ROW_SUBLANES, LANES), U32),
        compiler_params=_params("parallel"),
        name="pack_table",
    )(tab)


def _inproj_kernel(x_ref, g_ref, w_ref, lng_ref, lnb_ref, o_ref):
    x = x_ref[...]
    ms = jnp.mean(x * x, axis=-1, keepdims=True)
    xn = (x * lax.rsqrt(ms + EPS) * g_ref[...]).astype(BF16)

    def block(j):
        cols = slice(j * D_MODEL, (j + 1) * D_MODEL)
        return cols, jnp.dot(xn, w_ref[:, cols], preferred_element_type=F32)

    cols, acc = block(0)
    o_ref[:, cols] = (acc * QK_SCALE).astype(BF16)
    for j in (1, 2):
        cols, acc = block(j)
        o_ref[:, cols] = acc.astype(BF16)
    cols, acc = block(3)
    o_ref[:, cols] = _gelu(acc).astype(BF16)
    cols, acc = block(4)
    v = _gelu(acc)
    mu = jnp.mean(v, axis=-1, keepdims=True)
    var = jnp.mean(jnp.square(v - mu), axis=-1, keepdims=True)
    o_ref[:, cols] = ((v - mu) * lax.rsqrt(var + EPS) * lng_ref[...] + lnb_ref[...]).astype(BF16)
    for j in (5, 6):
        cols, acc = block(j)
        o_ref[:, cols] = (1.0 / (1.0 + jnp.exp(-acc))).astype(BF16)


def _inproj(x2, g1, w, lng, lnb, tm=256):
    t = x2.shape[0]
    assert w.shape[1] == 7 * D_MODEL
    vec = pl.BlockSpec((1, D_MODEL), lambda i: (0, 0))
    return pl.pallas_call(
        _inproj_kernel,
        grid=(t // tm,),
        in_specs=[pl.BlockSpec((tm, D_MODEL), lambda i: (i, 0)), vec,
                  pl.BlockSpec(w.shape, lambda i: (0, 0), pipeline_mode=pl.Buffered(1)), vec, vec],
        out_specs=pl.BlockSpec((tm, w.shape[1]), lambda i: (i, 0)),
        out_shape=jax.ShapeDtypeStruct((t, w.shape[1]), BF16),
        compiler_params=_params("parallel"),
        name="inproj",
    )(x2, g1, w, lng, lnb)


def _attn_kernel(q_ref, k_ref, v_ref, lq1_ref, lk1_ref, lq2_ref, lk2_ref, sg_ref, o_ref,
                 qbd_ref, vt_ref, acc_ref, pt_ref, *, tq, hp):
    i = pl.program_id(2)
    nkv = vt_ref.shape[1]
    hcols = lambda h: slice(h * LANES, (h + 1) * LANES)

    @pl.when(i == 0)
    def _():
        for h in range(hp):
            for j in range(nkv):
                vt_ref[h, j, :DA_DV, :] = v_ref[j * tq:(j + 1) * tq, hcols(h)].T
                vt_ref[h, j, DA_DV:, :] = jnp.ones((ONES_ROWS, tq), BF16)

    lane = lax.broadcasted_iota(I32, (tq, LANES), 1)
    for h in range(hp):
        q = q_ref[:, hcols(h)]
        zero = jnp.zeros_like(q)
        qbd_ref[h] = jnp.concatenate([jnp.where(lane < DA_DQK, q, zero), jnp.where(lane >= DA_DQK, q, zero)],
                                     axis=0)
        acc_ref[h] = jnp.zeros(acc_ref.shape[1:], F32)
        pt_ref[h] = jnp.zeros(pt_ref.shape[1:], BF16)

    def value_update(j, alphas):
        pvs = [jnp.dot(vt_ref[h, j], pt_ref[h], preferred_element_type=F32) for h in range(hp)]
        for h in range(hp):
            acc_ref[h] = alphas[h] * acc_ref[h] + pvs[h]

    def step(j, carry, mask):
        ms, alphas = carry
        kstart = pl.multiple_of(j * tq, tq)
        jp = jnp.maximum(j - 1, 0)
        sts, pvs = [], []
        for h in range(hp):
            sts.append(lax.dot_general(k_ref[pl.ds(kstart, tq), hcols(h)], qbd_ref[h], (((1,), (1,)), ((), ())),
                                       preferred_element_type=F32))
            pvs.append(jnp.dot(vt_ref[h, jp], pt_ref[h], preferred_element_type=F32))
        new_ms, new_alphas = [], []
        for h in range(hp):
            st = sts[h] if mask is None else jnp.where(mask, sts[h], -1e30)
            m_new = jnp.maximum(ms[h], jnp.max(st, axis=0, keepdims=True))
            new_alphas.append(jnp.exp2(ms[h] - m_new))
            new_ms.append(m_new)
            acc_ref[h] = alphas[h] * acc_ref[h] + pvs[h]
            pt_ref[h] = jnp.exp2(st - m_new).astype(BF16)
        return tuple(new_ms), tuple(new_alphas)

    init = (tuple(jnp.full((1, 2 * tq), -jnp.inf, F32) for _ in range(hp)),
            tuple(jnp.ones((1, 2 * tq), F32) for _ in range(hp)))
    carry = lax.fori_loop(0, i, lambda j, c: step(j, c, None), init)
    key = lax.broadcasted_iota(I32, (tq, 2 * tq), 0)
    col = lax.broadcasted_iota(I32, (tq, 2 * tq), 1)
    qpos = jnp.where(col >= tq, col - tq, col)
    _, alphas = step(i, carry, (key // CHUNK) <= (qpos // CHUNK))
    value_update(i, alphas)

    lam = (jnp.exp(jnp.sum(lq1_ref[...] * lk1_ref[...], axis=-1, keepdims=True))
           - jnp.exp(jnp.sum(lq2_ref[...] * lk2_ref[...], axis=-1, keepdims=True)) + LAMBDA_INIT)
    for h in range(hp):
        acc = acc_ref[h]
        on = acc[:DA_DV] / acc[DA_DV:DA_DV + 1]
        o = on[:, :tq] - lam * on[:, tq:]
        msq = jnp.mean(o * o, axis=0, keepdims=True)
        o = o * lax.rsqrt(msq + EPS) * sg_ref[...] * (1.0 - LAMBDA_INIT)
        o_ref[:, hcols(h)] = o.T.astype(BF16)


def _attention(proj3, lq1, lk1, lq2, lk2, subln_col, tq=256, hp=DA_HEADS):
    b, s, _ = proj3.shape
    ng = DA_HEADS // hp
    lam_spec = pl.BlockSpec((1, DA_DQK), lambda bi, g, i: (0, 0))
    return pl.pallas_call(
        functools.partial(_attn_kernel, tq=tq, hp=hp),
        grid=(b, ng, s // tq),
        in_specs=[
            pl.BlockSpec((None, tq, hp * LANES), lambda bi, g, i: (bi, i, g)),
            pl.BlockSpec((None, s, hp * LANES), lambda bi, g, i: (bi, 0, ng + g), pipeline_mode=pl.Buffered(1)),
            pl.BlockSpec((None, s, hp * LANES), lambda bi, g, i: (bi, 0, 2 * ng + g), pipeline_mode=pl.Buffered(1)),
            lam_spec, lam_spec, lam_spec, lam_spec,
            pl.BlockSpec((DA_DV, 1), lambda bi, g, i: (0, 0)),
        ],
        out_specs=pl.BlockSpec((None, tq, hp * DA_DV), lambda bi, g, i: (bi, i, g)),
        out_shape=jax.ShapeDtypeStruct((b, s, DA_HEADS * DA_DV), BF16),
        scratch_shapes=[pltpu.VMEM((hp, 2 * tq, LANES), BF16),
                        pltpu.VMEM((hp, s // tq, DA_DV + ONES_ROWS, tq), BF16),
                        pltpu.VMEM((hp, DA_DV + ONES_ROWS, 2 * tq), F32),
                        pltpu.VMEM((hp, tq, 2 * tq), BF16)],
        compiler_params=_params("parallel", "parallel", "arbitrary"),
        name="diff_attn",
    )(proj3, proj3, proj3, lq1, lk1, lq2, lk2, subln_col)


def _mix_kernel(su_ref, sv_ref, ga_ref, gb_ref, ya_ref, x_ref, sgw_ref, sgb_ref, wa_ref, wb_ref,
                wo_ref, g2_ref, wq_ref, h_ref, xn_ref, qp_ref, yb_ref, *, tm):
    r = lax.broadcasted_iota(I32, (SG_BLOCK, SG_BLOCK), 0)
    c = lax.broadcasted_iota(I32, (SG_BLOCK, SG_BLOCK), 1)
    causal = (r // CHUNK) >= (c // CHUNK)
    for g in range(SG_GROUPS):
        wm = jnp.where(causal, sgw_ref[g], 0.0).astype(BF16)
        cols = slice(g * SG_BLOCK, (g + 1) * SG_BLOCK)
        for n in range(tm // SG_BLOCK):
            rows = slice(n * SG_BLOCK, (n + 1) * SG_BLOCK)
            mixed = jnp.dot(wm, sv_ref[rows, cols], preferred_element_type=F32) + sgb_ref[g]
            yb_ref[rows, cols] = (su_ref[rows, cols].astype(F32) * mixed).astype(BF16)
    a = jnp.dot(ya_ref[...], wa_ref[...], preferred_element_type=F32)
    bm = jnp.dot(yb_ref[...], wb_ref[...], preferred_element_type=F32)
    merged = ga_ref[...].astype(F32) * a + gb_ref[...].astype(F32) * bm
    h = x_ref[...] + jnp.dot(merged.astype(BF16), wo_ref[...], preferred_element_type=F32)
    h_ref[...] = h
    ms = jnp.mean(h * h, axis=-1, keepdims=True)
    xn = h * lax.rsqrt(ms + EPS) * g2_ref[...]
    xn_ref[...] = xn
    qp_ref[...] = jnp.dot(xn.astype(BF16), wq_ref[...], preferred_element_type=F32)


def _mix(proj, ya, x2, sgw, sgb, wa, wb, wo, g2, wq, tm=256):
    t = x2.shape[0]
    nq = wq.shape[1]
    full = lambda shape: pl.BlockSpec(shape, lambda i: (0,) * len(shape))
    col = lambda cb: pl.BlockSpec((tm, D_MODEL), lambda i: (i, cb))
    return pl.pallas_call(
        functools.partial(_mix_kernel, tm=tm),
        grid=(t // tm,),
        in_specs=[col(3), col(4), col(5), col(6), col(0), col(0),
                  full((SG_GROUPS, SG_BLOCK, SG_BLOCK)), full((SG_GROUPS, SG_BLOCK, 1)),
                  full((D_MODEL, D_MODEL)), full((D_MODEL, D_MODEL)), full((D_MODEL, D_MODEL)),
                  full((1, D_MODEL)), full((D_MODEL, nq))],
        out_specs=[col(0), col(0), pl.BlockSpec((tm, nq), lambda i: (i, 0))],
        out_shape=[jax.ShapeDtypeStruct((t, D_MODEL), F32), jax.ShapeDtypeStruct((t, D_MODEL), F32),
                   jax.ShapeDtypeStruct((t, nq), F32)],
        scratch_shapes=[pltpu.VMEM((tm, D_MODEL), BF16)],
        compiler_params=_params("parallel"),
        name="gmlp_merge",
    )(proj, proj, proj, proj, ya, x2, sgw, sgb, wa, wb, wo, g2, wq)


TOPK_LEVELS = 4
TOPK_LEVEL_ROWS = N_KEYS // TOPK_LEVELS
_GRID_GROUPS = ((0, 0, 8), (0, 8, 8)) + tuple((a, 0, min(8, PEER_TOPK // (a + 1))) for a in range(1, 8))


def _topk_kernel(qp_ref, sk_ref, idx_ref, gate_ref, sv_ref, si_ref, ts_ref, te_ref, *, tt):
    neg = -jnp.inf
    iota_l = lax.broadcasted_iota(I32, (TOPK_LEVEL_ROWS, tt), 0).astype(F32)
    ncand = PEER_TOPK * PEER_TOPK
    sub8 = lax.broadcasted_iota(I32, (8, tt), 0).astype(F32)
    for h in range(PEER_HEADS):
        for p in range(2):
            c0 = (h * 2 + p) * D_KEY_HALF
            qh = qp_ref[:, c0:c0 + D_KEY_HALF].astype(BF16)
            s = lax.dot_general(sk_ref[p].astype(BF16), qh, (((1,), (1,)), ((), ())),
                                preferred_element_type=F32)
            vals = [s[d * TOPK_LEVEL_ROWS:(d + 1) * TOPK_LEVEL_ROWS, :] for d in range(TOPK_LEVELS)]
            keys = [iota_l + float(d * TOPK_LEVEL_ROWS) for d in range(TOPK_LEVELS)]
            for hi in range(TOPK_LEVELS - 1, 0, -1):
                for d in range(hi):
                    swap = vals[d + 1] > vals[d]
                    vals[d], vals[d + 1] = (jnp.where(swap, vals[d + 1], vals[d]),
                                            jnp.where(swap, vals[d], vals[d + 1]))
                    keys[d], keys[d + 1] = (jnp.where(swap, keys[d + 1], keys[d]),
                                            jnp.where(swap, keys[d], keys[d + 1]))
            for k in range(PEER_TOPK):
                m = jnp.max(vals[0], axis=0, keepdims=True)
                ii = jnp.min(jnp.where(vals[0] == m, keys[0], float(N_KEYS)), axis=0, keepdims=True)
                pop = keys[0] == ii
                for d in range(TOPK_LEVELS - 1):
                    vals[d] = jnp.where(pop, vals[d + 1], vals[d])
                    keys[d] = jnp.where(pop, keys[d + 1], keys[d])
                vals[-1] = jnp.where(pop, neg, vals[-1])
                sv_ref[p, k:k + 1, :] = m
                si_ref[p, k:k + 1, :] = ii
        cands, cposs, cidxs = [], [], []
        for a, b0, nvalid in _GRID_GROUPS:
            c = sv_ref[0, a:a + 1, :] + sv_ref[1, b0:b0 + 8, :]
            cands.append(c if nvalid == 8 else jnp.where(sub8 < float(nvalid), c, neg))
            cposs.append(sub8 + float(a * PEER_TOPK + b0))
            cidxs.append(si_ref[0, a:a + 1, :] * float(N_KEYS) + si_ref[1, b0:b0 + 8, :])
        cands.append(sv_ref[0, 8:16, :] + sv_ref[1, 0:1, :])
        cposs.append((sub8 + 8.0) * float(PEER_TOPK))
        cidxs.append(si_ref[0, 8:16, :] * float(N_KEYS) + si_ref[1, 0:1, :])
        cand = jnp.concatenate(cands, axis=0)
        cpos = jnp.concatenate(cposs, axis=0)
        cidx = jnp.concatenate(cidxs, axis=0)
        for k in range(PEER_TOPK):
            m = jnp.max(cand, axis=0, keepdims=True)
            pos = jnp.min(jnp.where(cand == m, cpos, float(ncand)), axis=0, keepdims=True)
            sel = cpos == pos
            e = jnp.max(jnp.where(sel, cidx, -1.0), axis=0, keepdims=True)
            cand = jnp.where(sel, neg, cand)
            ts_ref[h * PEER_TOPK + k:h * PEER_TOPK + k + 1, :] = m
            te_ref[h * PEER_TOPK + k:h * PEER_TOPK + k + 1, :] = e
        rows = slice(h * PEER_TOPK, (h + 1) * PEER_TOPK)
        ts = ts_ref[rows, :]
        ex = jnp.exp(ts - jnp.max(ts, axis=0, keepdims=True))
        gate_ref[rows, :] = ex / jnp.sum(ex, axis=0, keepdims=True)
    idx_ref[...] = (te_ref[...] * float(ROW_SUBLANES)).T.astype(I32)


def _topk(qp, subkeys, tt=LANES):
    t = qp.shape[0]
    nblk = t // tt
    out_spec = pl.BlockSpec((None, N_SEL, tt), lambda i: (i, 0, 0))
    return pl.pallas_call(
        functools.partial(_topk_kernel, tt=tt),
        grid=(nblk,),
        in_specs=[pl.BlockSpec((tt, qp.shape[1]), lambda i: (i, 0)),
                  pl.BlockSpec((2, N_KEYS, D_KEY_HALF), lambda i: (0, 0, 0))],
        out_specs=[pl.BlockSpec((None, tt, N_SEL), lambda i: (i, 0, 0)), out_spec],
        out_shape=[jax.ShapeDtypeStruct((nblk, tt, N_SEL), I32),
                   jax.ShapeDtypeStruct((nblk, N_SEL, tt), F32)],
        scratch_shapes=[pltpu.VMEM((2, PEER_TOPK, tt), F32), pltpu.VMEM((2, PEER_TOPK, tt), F32),
                        pltpu.VMEM((N_SEL, tt), F32), pltpu.VMEM((N_SEL, tt), F32)],
        compiler_params=_params("parallel"),
        name="peer_topk",
    )(qp, subkeys)


def _gather_rows(idx_ref, tab_ref, tile_ref, t):
    for j in range(N_SEL):
        start = pl.multiple_of(idx_ref[t, j], ROW_SUBLANES)
        tile_ref[pl.ds(j, ROW_SUBLANES, stride=TILE_STRIDE), :] = tab_ref[pl.ds(start, ROW_SUBLANES), :]


def _unpack(tile_ref, s):
    g = tile_ref[s * TILE_STRIDE:s * TILE_STRIDE + N_SEL, :]
    lo = pltpu.unpack_elementwise(g, index=0, packed_dtype=BF16, unpacked_dtype=F32)
    hi = pltpu.unpack_elementwise(g, index=1, packed_dtype=BF16, unpacked_dtype=F32)
    return lo, hi


def _next_token(t, tb):
    return jnp.minimum(t, tb - 1)


def _peer_u_kernel(idx_ref, x_ref, gate_ref, tab_ref, w_ref, tile_a, tile_b, hid_ref, part_ref, *, tb):
    lane = lax.broadcasted_iota(I32, (N_SEL, tb), 1)

    def products(tile_ref, t, slot):
        part = jnp.zeros((N_SEL, LANES), F32)
        for s in range(ROW_SUBLANES):
            lo, hi = _unpack(tile_ref, s)
            part = part + lo * x_ref[pl.ds(t, 1), s, :] + hi * x_ref[pl.ds(t, 1), ROW_SUBLANES + s, :]
        part_ref[slot] = part

    def insert(slot, t):
        col = jnp.sum(part_ref[slot], axis=1, keepdims=True)
        hid_ref[...] = jnp.where(lane == t, col, hid_ref[...])

    part_ref[...] = jnp.zeros(part_ref.shape, F32)
    hid_ref[...] = jnp.zeros(hid_ref.shape, F32)
    _gather_rows(idx_ref, tab_ref, tile_a, 0)

    def body(p, carry):
        t = 2 * p
        insert(0, t - 2)
        insert(1, t - 1)
        _gather_rows(idx_ref, tab_ref, tile_b, t + 1)
        products(tile_a, t, 0)
        _gather_rows(idx_ref, tab_ref, tile_a, _next_token(t + 2, tb))
        products(tile_b, t + 1, 1)
        return carry

    lax.fori_loop(0, tb // 2, body, 0)
    insert(0, tb - 2)
    insert(1, tb - 1)
    w_ref[...] = gate_ref[...] * _gelu(hid_ref[...])


def _peer_v_kernel(idx_ref, w_ref, tab_ref, h_ref, g_ref, o_ref, tile_a, tile_b, y_ref, *, tb):
    lane = lax.broadcasted_iota(I32, (N_SEL, tb), 1)

    def weighted_sum(tile_ref, t):
        wcol = jnp.sum(jnp.where(lane == t, w_ref[...], 0.0), axis=1, keepdims=True)
        for s in range(ROW_SUBLANES):
            lo, hi = _unpack(tile_ref, s)
            y_ref[pl.ds(t, 1), s, :] = jnp.sum(lo * wcol, axis=0, keepdims=True)
            y_ref[pl.ds(t, 1), ROW_SUBLANES + s, :] = jnp.sum(hi * wcol, axis=0, keepdims=True)

    _gather_rows(idx_ref, tab_ref, tile_a, 0)

    def body(p, carry):
        t = 2 * p
        _gather_rows(idx_ref, tab_ref, tile_b, t + 1)
        weighted_sum(tile_a, t)
        _gather_rows(idx_ref, tab_ref, tile_a, _next_token(t + 2, tb))
        weighted_sum(tile_b, t + 1)
        return carry

    lax.fori_loop(0, tb // 2, body, 0)
    nchunk = 2 * ROW_SUBLANES
    hs = [h_ref[:, c * LANES:(c + 1) * LANES] + y_ref[:, c, :] for c in range(nchunk)]
    ms = sum(jnp.sum(hc * hc, axis=-1, keepdims=True) for hc in hs) * (1.0 / D_MODEL)
    scale = lax.rsqrt(ms + EPS)
    for c in range(nchunk):
        o_ref[:, c * LANES:(c + 1) * LANES] = hs[c] * scale * g_ref[:, c * LANES:(c + 1) * LANES]


def _peer_specs(tb):
    idx_spec = pl.BlockSpec((None, tb, N_SEL), lambda i: (i, 0, 0), memory_space=pltpu.SMEM)
    blk_spec = pl.BlockSpec((None, N_SEL, tb), lambda i: (i, 0, 0))
    tok_spec = pl.BlockSpec((tb, 2 * ROW_SUBLANES, LANES), lambda i: (i, 0, 0))
    tab_spec = pl.BlockSpec(memory_space=pltpu.VMEM)
    tile = pltpu.VMEM((ROW_SUBLANES * TILE_STRIDE, LANES), U32)
    return idx_spec, blk_spec, tok_spec, tab_spec, [tile, tile]


def _peer_u(idx, xn3, gate, tab, tb=LANES):
    nblk = idx.shape[0]
    idx_spec, blk_spec, tok_spec, tab_spec, tiles = _peer_specs(tb)
    return pl.pallas_call(
        functools.partial(_peer_u_kernel, tb=tb),
        grid=(nblk,),
        in_specs=[idx_spec, tok_spec, blk_spec, tab_spec],
        out_specs=blk_spec,
        out_shape=jax.ShapeDtypeStruct((nblk, N_SEL, tb), F32),
        scratch_shapes=tiles + [pltpu.VMEM((N_SEL, tb), F32), pltpu.VMEM((2, N_SEL, LANES), F32)],
        compiler_params=_params("arbitrary"),
        name="peer_u",
    )(idx, xn3, gate, tab)


def _peer_v(idx, w, tab, h, g, tb=LANES):
    nblk = idx.shape[0]
    idx_spec, blk_spec, _, tab_spec, tiles = _peer_specs(tb)
    row_spec = pl.BlockSpec((tb, D_MODEL), lambda i: (i, 0))
    return pl.pallas_call(
        functools.partial(_peer_v_kernel, tb=tb),
        grid=(nblk,),
        in_specs=[idx_spec, blk_spec, tab_spec, row_spec, pl.BlockSpec((1, D_MODEL), lambda i: (0, 0))],
        out_specs=row_spec,
        out_shape=jax.ShapeDtypeStruct((nblk * tb, D_MODEL), F32),
        scratch_shapes=tiles + [pltpu.VMEM((tb, 2 * ROW_SUBLANES, LANES), F32)],
        compiler_params=_params("arbitrary"),
        name="peer_v",
    )(idx, w, tab, h, g)


def _head_major_columns():
    h = jnp.arange(DA_HEADS)[:, None, None]
    m = jnp.arange(2)[None, :, None]
    d = jnp.arange(DA_DQK)[None, None, :]
    return (m * (DA_HEADS * DA_DQK) + h * DA_DQK + d).reshape(-1)


def kernel(x, norm1_g, w_in, lambda_q1, lambda_k1, lambda_q2, lambda_k2, da_subln_g, sg_ln_g, sg_ln_b,
           sg_w, sg_b, w_branch_attn, w_branch_sg, w_out, norm2_g, peer_w_query, peer_subkeys, peer_u,
           peer_v, final_g):
    b, s, d = x.shape
    t = b * s
    assert w_in.shape[0] == 1, "single-layer block"
    x2 = x.reshape(t, d)
    row = lambda a: a.reshape(1, -1)

    perm = _head_major_columns()
    qk = 2 * DA_HEADS * DA_DQK
    w = w_in[0]
    w = jnp.concatenate([w[:, :qk][:, perm], w[:, qk:2 * qk][:, perm], w[:, 2 * qk:]], axis=1).astype(BF16)

    proj = _inproj(x2, row(norm1_g[0]), w, row(sg_ln_g[0]), row(sg_ln_b[0]))
    ya = _attention(proj.reshape(b, s, -1), row(lambda_q1[0]), row(lambda_k1[0]), row(lambda_q2[0]),
                    row(lambda_k2[0]), da_subln_g[0].reshape(-1, 1))
    h, xn, qp = _mix(proj, ya.reshape(t, -1), x2, sg_w[0], sg_b[0][:, :, None],
                     w_branch_attn[0].astype(BF16), w_branch_sg[0].astype(BF16), w_out[0].astype(BF16),
                     row(norm2_g[0]), peer_w_query[0].astype(BF16))
    idx, gate = _topk(qp, peer_subkeys[0])
    wts = _peer_u(idx, xn.reshape(t, 2 * ROW_SUBLANES, LANES), gate, _pack_table(peer_u[0]))
    out = _peer_v(idx, wts, _pack_table(peer_v[0]), h, row(final_g))
    return out.reshape(b, s, d)
```

```python
import functools
import math

import jax
import jax.numpy as jnp
from jax import lax
from jax.experimental import pallas as pl
from jax.experimental.pallas import tpu as pltpu

F32 = jnp.float32
BF16 = jnp.bfloat16
I32 = jnp.int32
U32 = jnp.uint32

D_MODEL = 1024
CHUNK = 64
EPS = 1e-6
DA_HEADS = 8
DA_DQK = 64
DA_DV = 128
SG_BLOCK = 128
SG_GROUPS = 8
PEER_HEADS = 8
N_KEYS = 128
PEER_TOPK = 16
D_KEY_HALF = 128
N_SEL = PEER_HEADS * PEER_TOPK
LANES = 128
WORDS_PER_ROW = D_MODEL // 2
ROW_SUBLANES = WORDS_PER_ROW // LANES
TILE_STRIDE = N_SEL + 8
VMEM_LIMIT = 56 * 1024 * 1024
LAMBDA_INIT = 0.8 - 0.6 * math.exp(-0.3 * 0)
SQRT_HALF = 0.7071067811865476
QK_SCALE = DA_DQK ** -0.5 * math.log2(math.e)
ONES_ROWS = 16


def _gelu(x):
    return 0.5 * x * (1.0 + lax.erf(x * SQRT_HALF))


def _params(*sem):
    return pltpu.CompilerParams(dimension_semantics=sem, vmem_limit_bytes=VMEM_LIMIT)


def _pack_kernel(t_ref, o_ref):
    t = t_ref[...]
    words = pltpu.pack_elementwise([t[:, :WORDS_PER_ROW], t[:, WORDS_PER_ROW:]], packed_dtype=BF16)
    for s in range(ROW_SUBLANES):
        o_ref[pl.ds(s, t.shape[0], stride=ROW_SUBLANES), :] = words[:, s * LANES:(s + 1) * LANES]


def _pack_table(tab, rows=512):
    n = tab.shape[0]
    return pl.pallas_call(
        _pack_kernel,
        grid=(n // rows,),
        in_specs=[pl.BlockSpec((rows, D_MODEL), lambda i: (i, 0))],
        out_specs=pl.BlockSpec((rows * ROW_SUBLANES, LANES), lambda i: (i, 0)),
        out_shape=jax.ShapeDtypeStruct((n * ROW_SUBLANES, LANES), U32),
        compiler_params=_params("parallel"),
        name="pack_table",
    )(tab)


def _inproj_kernel(x_ref, g_ref, w_ref, lng_ref, lnb_ref, o_ref):
    x = x_ref[...]
    ms = jnp.mean(x * x, axis=-1, keepdims=True)
    xn = (x * lax.rsqrt(ms + EPS) * g_ref[...]).astype(BF16)

    def block(j):
        cols = slice(j * D_MODEL, (j + 1) * D_MODEL)
        return cols, jnp.dot(xn, w_ref[:, cols], preferred_element_type=F32)

    cols, acc = block(0)
    o_ref[:, cols] = (acc * QK_SCALE).astype(BF16)
    for j in (1, 2):
        cols, acc = block(j)
        o_ref[:, cols] = acc.astype(BF16)
    cols, acc = block(3)
    o_ref[:, cols] = _gelu(acc).astype(BF16)
    cols, acc = block(4)
    v = _gelu(acc)
    mu = jnp.mean(v, axis=-1, keepdims=True)
    var = jnp.mean(jnp.square(v - mu), axis=-1, keepdims=True)
    o_ref[:, cols] = ((v - mu) * lax.rsqrt(var + EPS) * lng_ref[...] + lnb_ref[...]).astype(BF16)
    for j in (5, 6):
        cols, acc = block(j)
        o_ref[:, cols] = (1.0 / (1.0 + jnp.exp(-acc))).astype(BF16)


def _inproj(x2, g1, w, lng, lnb, tm=256):
    t = x2.shape[0]
    assert w.shape[1] == 7 * D_MODEL
    vec = pl.BlockSpec((1, D_MODEL), lambda i: (0, 0))
    return pl.pallas_call(
        _inproj_kernel,
        grid=(t // tm,),
        in_specs=[pl.BlockSpec((tm, D_MODEL), lambda i: (i, 0)), vec,
                  pl.BlockSpec(w.shape, lambda i: (0, 0), pipeline_mode=pl.Buffered(1)), vec, vec],
        out_specs=pl.BlockSpec((tm, w.shape[1]), lambda i: (i, 0)),
        out_shape=jax.ShapeDtypeStruct((t, w.shape[1]), BF16),
        compiler_params=_params("parallel"),
        name="inproj",
    )(x2, g1, w, lng, lnb)


def _attn_kernel(q_ref, k_ref, v_ref, lq1_ref, lk1_ref, lq2_ref, lk2_ref, sg_ref, o_ref,
                 qbd_ref, vt_ref, acc_ref, pt_ref, *, tq, hp):
    i = pl.program_id(2)
    nkv = vt_ref.shape[1]
    hcols = lambda h: slice(h * LANES, (h + 1) * LANES)

    @pl.when(i == 0)
    def _():
        for h in range(hp):
            for j in range(nkv):
                vt_ref[h, j, :DA_DV, :] = v_ref[j * tq:(j + 1) * tq, hcols(h)].T
                vt_ref[h, j, DA_DV:, :] = jnp.ones((ONES_ROWS, tq), BF16)

    lane = lax.broadcasted_iota(I32, (tq, LANES), 1)
    for h in range(hp):
        q = q_ref[:, hcols(h)]
        zero = jnp.zeros_like(q)
        qbd_ref[h] = jnp.concatenate([jnp.where(lane < DA_DQK, q, zero), jnp.where(lane >= DA_DQK, q, zero)],
                                     axis=0)
        acc_ref[h] = jnp.zeros(acc_ref.shape[1:], F32)
        pt_ref[h] = jnp.zeros(pt_ref.shape[1:], BF16)

    def value_update(j, alphas):
        pvs = [jnp.dot(vt_ref[h, j], pt_ref[h], preferred_element_type=F32) for h in range(hp)]
        for h in range(hp):
            acc_ref[h] = alphas[h] * acc_ref[h] + pvs[h]

    def step(j, carry, mask):
        ms, alphas = carry
        kstart = pl.multiple_of(j * tq, tq)
        jp = jnp.maximum(j - 1, 0)
        sts, pvs = [], []
        for h in range(hp):
            sts.append(lax.dot_general(k_ref[pl.ds(kstart, tq), hcols(h)], qbd_ref[h], (((1,), (1,)), ((), ())),
                                       preferred_element_type=F32))
            pvs.append(jnp.dot(vt_ref[h, jp], pt_ref[h], preferred_element_type=F32))
        new_ms, new_alphas = [], []
        for h in range(hp):
            st = sts[h] if mask is None else jnp.where(mask, sts[h], -1e30)
            m_new = jnp.maximum(ms[h], jnp.max(st, axis=0, keepdims=True))
            new_alphas.append(jnp.exp2(ms[h] - m_new))
            new_ms.append(m_new)
            acc_ref[h] = alphas[h] * acc_ref[h] + pvs[h]
            pt_ref[h] = jnp.exp2(st - m_new).astype(BF16)
        return tuple(new_ms), tuple(new_alphas)

    init = (tuple(jnp.full((1, 2 * tq), -jnp.inf, F32) for _ in range(hp)),
            tuple(jnp.ones((1, 2 * tq), F32) for _ in range(hp)))
    carry = lax.fori_loop(0, i, lambda j, c: step(j, c, None), init)
    key = lax.broadcasted_iota(I32, (tq, 2 * tq), 0)
    col = lax.broadcasted_iota(I32, (tq, 2 * tq), 1)
    qpos = jnp.where(col >= tq, col - tq, col)
    _, alphas = step(i, carry, (key // CHUNK) <= (qpos // CHUNK))
    value_update(i, alphas)

    lam = (jnp.exp(jnp.sum(lq1_ref[...] * lk1_ref[...], axis=-1, keepdims=True))
           - jnp.exp(jnp.sum(lq2_ref[...] * lk2_ref[...], axis=-1, keepdims=True)) + LAMBDA_INIT)
    for h in range(hp):
        acc = acc_ref[h]
        on = acc[:DA_DV] / acc[DA_DV:DA_DV + 1]
        o = on[:, :tq] - lam * on[:, tq:]
        msq = jnp.mean(o * o, axis=0, keepdims=True)
        o = o * lax.rsqrt(msq + EPS) * sg_ref[...] * (1.0 - LAMBDA_INIT)
        o_ref[:, hcols(h)] = o.T.astype(BF16)


def _attention(proj3, lq1, lk1, lq2, lk2, subln_col, tq=256, hp=DA_HEADS):
    b, s, _ = proj3.shape
    ng = DA_HEADS // hp
    lam_spec = pl.BlockSpec((1, DA_DQK), lambda bi, g, i: (0, 0))
    return pl.pallas_call(
        functools.partial(_attn_kernel, tq=tq, hp=hp),
        grid=(b, ng, s // tq),
        in_specs=[
            pl.BlockSpec((None, tq, hp * LANES), lambda bi, g, i: (bi, i, g)),
            pl.BlockSpec((None, s, hp * LANES), lambda bi, g, i: (bi, 0, ng + g), pipeline_mode=pl.Buffered(1)),
            pl.BlockSpec((None, s, hp * LANES), lambda bi, g, i: (bi, 0, 2 * ng + g), pipeline_mode=pl.Buffered(1)),
            lam_spec, lam_spec, lam_spec, lam_spec,
            pl.BlockSpec((DA_DV, 1), lambda bi, g, i: (0, 0)),
        ],
        out_specs=pl.BlockSpec((None, tq, hp * DA_DV), lambda bi, g, i: (bi, i, g)),
        out_shape=jax.ShapeDtypeStruct((b, s, DA_HEADS * DA_DV), BF16),
        scratch_shapes=[pltpu.VMEM((hp, 2 * tq, LANES), BF16),
                        pltpu.VMEM((hp, s // tq, DA_DV + ONES_ROWS, tq), BF16),
                        pltpu.VMEM((hp, DA_DV + ONES_ROWS, 2 * tq), F32),
                        pltpu.VMEM((hp, tq, 2 * tq), BF16)],
        compiler_params=_params("parallel", "parallel", "arbitrary"),
        name="diff_attn",
    )(proj3, proj3, proj3, lq1, lk1, lq2, lk2, subln_col)


def _mix_kernel(su_ref, sv_ref, ga_ref, gb_ref, ya_ref, x_ref, sgw_ref, sgb_ref, wa_ref, wb_ref,
                wo_ref, g2_ref, wq_ref, h_ref, xn_ref, qp_ref, yb_ref, *, tm):
    r = lax.broadcasted_iota(I32, (SG_BLOCK, SG_BLOCK), 0)
    c = lax.broadcasted_iota(I32, (SG_BLOCK, SG_BLOCK), 1)
    causal = (r // CHUNK) >= (c // CHUNK)
    for g in range(SG_GROUPS):
        wm = jnp.where(causal, sgw_ref[g], 0.0).astype(BF16)
        cols = slice(g * SG_BLOCK, (g + 1) * SG_BLOCK)
        for n in range(tm // SG_BLOCK):
            rows = slice(n * SG_BLOCK, (n + 1) * SG_BLOCK)
            mixed = jnp.dot(wm, sv_ref[rows, cols], preferred_element_type=F32) + sgb_ref[g]
            yb_ref[rows, cols] = (su_ref[rows, cols].astype(F32) * mixed).astype(BF16)
    a = jnp.dot(ya_ref[...], wa_ref[...], preferred_element_type=F32)
    bm = jnp.dot(yb_ref[...], wb_ref[...], preferred_element_type=F32)
    merged = ga_ref[...].astype(F32) * a + gb_ref[...].astype(F32) * bm
    h = x_ref[...] + jnp.dot(merged.astype(BF16), wo_ref[...], preferred_element_type=F32)
    h_ref[...] = h
    ms = jnp.mean(h * h, axis=-1, keepdims=True)
    xn = h * lax.rsqrt(ms + EPS) * g2_ref[...]
    xn_ref[...] = xn
    qp_ref[...] = jnp.dot(xn.astype(BF16), wq_ref[...], preferred_element_type=F32)


def _mix(proj, ya, x2, sgw, sgb, wa, wb, wo, g2, wq, tm=256):
    t = x2.shape[0]
    nq = wq.shape[1]
    full = lambda shape: pl.BlockSpec(shape, lambda i: (0,) * len(shape))
    col = lambda cb: pl.BlockSpec((tm, D_MODEL), lambda i: (i, cb))
    return pl.pallas_call(
        functools.partial(_mix_kernel, tm=tm),
        grid=(t // tm,),
        in_specs=[col(3), col(4), col(5), col(6), col(0), col(0),
                  full((SG_GROUPS, SG_BLOCK, SG_BLOCK)), full((SG_GROUPS, SG_BLOCK, 1)),
                  full((D_MODEL, D_MODEL)), full((D_MODEL, D_MODEL)), full((D_MODEL, D_MODEL)),
                  full((1, D_MODEL)), full((D_MODEL, nq))],
        out_specs=[col(0), col(0), pl.BlockSpec((tm, nq), lambda i: (i, 0))],
        out_shape=[jax.ShapeDtypeStruct((t, D_MODEL), F32), jax.ShapeDtypeStruct((t, D_MODEL), F32),
                   jax.ShapeDtypeStruct((t, nq), F32)],
        scratch_shapes=[pltpu.VMEM((tm, D_MODEL), BF16)],
        compiler_params=_params("parallel"),
        name="gmlp_merge",
    )(proj, proj, proj, proj, ya, x2, sgw, sgb, wa, wb, wo, g2, wq)


TOPK_LEVELS = 8
TOPK_LEVEL_ROWS = N_KEYS // TOPK_LEVELS
_GRID_GROUPS = ((0, 0, 8), (0, 8, 8)) + tuple((a, 0, min(8, PEER_TOPK // (a + 1))) for a in range(1, 8))


def _topk_kernel(qp_ref, sk_ref, idx_ref, gate_ref, sv_ref, si_ref, ts_ref, te_ref, *, tt):
    neg = -jnp.inf
    iota_l = lax.broadcasted_iota(I32, (TOPK_LEVEL_ROWS, tt), 0).astype(F32)
    ncand = PEER_TOPK * PEER_TOPK
    sub8 = lax.broadcasted_iota(I32, (8, tt), 0).astype(F32)
    for h in range(PEER_HEADS):
        for p in range(2):
            c0 = (h * 2 + p) * D_KEY_HALF
            qh = qp_ref[:, c0:c0 + D_KEY_HALF].astype(BF16)
            s = lax.dot_general(sk_ref[p].astype(BF16), qh, (((1,), (1,)), ((), ())),
                                preferred_element_type=F32)
            vals = [s[d * TOPK_LEVEL_ROWS:(d + 1) * TOPK_LEVEL_ROWS, :] for d in range(TOPK_LEVELS)]
            keys = [iota_l + float(d * TOPK_LEVEL_ROWS) for d in range(TOPK_LEVELS)]
            for hi in range(TOPK_LEVELS - 1, 0, -1):
                for d in range(hi):
                    swap = vals[d + 1] > vals[d]
                    vals[d], vals[d + 1] = (jnp.where(swap, vals[d + 1], vals[d]),
                                            jnp.where(swap, vals[d], vals[d + 1]))
                    keys[d], keys[d + 1] = (jnp.where(swap, keys[d + 1], keys[d]),
                                            jnp.where(swap, keys[d], keys[d + 1]))
            for k in range(PEER_TOPK):
                m = jnp.max(vals[0], axis=0, keepdims=True)
                ii = jnp.min(jnp.where(vals[0] == m, keys[0], float(N_KEYS)), axis=0, keepdims=True)
                pop = keys[0] == ii
                for d in range(TOPK_LEVELS - 1):
                    vals[d] = jnp.where(pop, vals[d + 1], vals[d])
                    keys[d] = jnp.where(pop, keys[d + 1], keys[d])
                vals[-1] = jnp.where(pop, neg, vals[-1])
                sv_ref[p, k:k + 1, :] = m
                si_ref[p, k:k + 1, :] = ii
        cands, cposs, cidxs = [], [], []
        for a, b0, nvalid in _GRID_GROUPS:
            c = sv_ref[0, a:a + 1, :] + sv_ref[1, b0:b0 + 8, :]
            cands.append(c if nvalid == 8 else jnp.where(sub8 < float(nvalid), c, neg))
            cposs.append(sub8 + float(a * PEER_TOPK + b0))
            cidxs.append(si_ref[0, a:a + 1, :] * float(N_KEYS) + si_ref[1, b0:b0 + 8, :])
        cands.append(sv_ref[0, 8:16, :] + sv_ref[1, 0:1, :])
        cposs.append((sub8 + 8.0) * float(PEER_TOPK))
        cidxs.append(si_ref[0, 8:16, :] * float(N_KEYS) + si_ref[1, 0:1, :])
        cand = jnp.concatenate(cands, axis=0)
        cpos = jnp.concatenate(cposs, axis=0)
        cidx = jnp.concatenate(cidxs, axis=0)
        for k in range(PEER_TOPK):
            m = jnp.max(cand, axis=0, keepdims=True)
            pos = jnp.min(jnp.where(cand == m, cpos, float(ncand)), axis=0, keepdims=True)
            sel = cpos == pos
            e = jnp.max(jnp.where(sel, cidx, -1.0), axis=0, keepdims=True)
            cand = jnp.where(sel, neg, cand)
            ts_ref[h * PEER_TOPK + k:h * PEER_TOPK + k + 1, :] = m
            te_ref[h * PEER_TOPK + k:h * PEER_TOPK + k + 1, :] = e
        rows = slice(h * PEER_TOPK, (h + 1) * PEER_TOPK)
        ts = ts_ref[rows, :]
        ex = jnp.exp(ts - jnp.max(ts, axis=0, keepdims=True))
        gate_ref[rows, :] = ex / jnp.sum(ex, axis=0, keepdims=True)
    idx_ref[...] = (te_ref[...] * float(ROW_SUBLANES)).T.astype(I32)


def _topk(qp, subkeys, tt=LANES):
    t = qp.shape[0]
    nblk = t // tt
    out_spec = pl.BlockSpec((None, N_SEL, tt), lambda i: (i, 0, 0))
    return pl.pallas_call(
        functools.partial(_topk_kernel, tt=tt),
        grid=(nblk,),
        in_specs=[pl.BlockSpec((tt, qp.shape[1]), lambda i: (i, 0)),
                  pl.BlockSpec((2, N_KEYS, D_KEY_HALF), lambda i: (0, 0, 0))],
        out_specs=[pl.BlockSpec((None, tt, N_SEL), lambda i: (i, 0, 0)), out_spec],
        out_shape=[jax.ShapeDtypeStruct((nblk, tt, N_SEL), I32),
                   jax.ShapeDtypeStruct((nblk, N_SEL, tt), F32)],
        scratch_shapes=[pltpu.VMEM((2, PEER_TOPK, tt), F32), pltpu.VMEM((2, PEER_TOPK, tt), F32),
                        pltpu.VMEM((N_SEL, tt), F32), pltpu.VMEM((N_SEL, tt), F32)],
        compiler_params=_params("parallel"),
        name="peer_topk",
    )(qp, subkeys)


def _gather_rows(idx_ref, tab_ref, tile_ref, t):
    for j in range(N_SEL):
        start = pl.multiple_of(idx_ref[t, j], ROW_SUBLANES)
        tile_ref[pl.ds(j, ROW_SUBLANES, stride=TILE_STRIDE), :] = tab_ref[pl.ds(start, ROW_SUBLANES), :]


def _unpack(tile_ref, s):
    g = tile_ref[s * TILE_STRIDE:s * TILE_STRIDE + N_SEL, :]
    lo = pltpu.unpack_elementwise(g, index=0, packed_dtype=BF16, unpacked_dtype=F32)
    hi = pltpu.unpack_elementwise(g, index=1, packed_dtype=BF16, unpacked_dtype=F32)
    return lo, hi


def _next_token(t, tb):
    return jnp.minimum(t, tb - 1)


def _peer_u_kernel(idx_ref, x_ref, gate_ref, tab_ref, w_ref, tile_a, tile_b, hid_ref, part_ref, *, tb):
    lane = lax.broadcasted_iota(I32, (N_SEL, tb), 1)

    def products(tile_ref, t, slot):
        part = jnp.zeros((N_SEL, LANES), F32)
        for s in range(ROW_SUBLANES):
            lo, hi = _unpack(tile_ref, s)
            part = part + lo * x_ref[pl.ds(t, 1), s, :] + hi * x_ref[pl.ds(t, 1), ROW_SUBLANES + s, :]
        part_ref[slot] = part

    def insert(slot, t):
        col = jnp.sum(part_ref[slot], axis=1, keepdims=True)
        hid_ref[...] = jnp.where(lane == t, col, hid_ref[...])

    part_ref[...] = jnp.zeros(part_ref.shape, F32)
    hid_ref[...] = jnp.zeros(hid_ref.shape, F32)
    _gather_rows(idx_ref, tab_ref, tile_a, 0)

    def body(p, carry):
        t = 2 * p
        insert(0, t - 2)
        insert(1, t - 1)
        _gather_rows(idx_ref, tab_ref, tile_b, t + 1)
        products(tile_a, t, 0)
        _gather_rows(idx_ref, tab_ref, tile_a, _next_token(t + 2, tb))
        products(tile_b, t + 1, 1)
        return carry

    lax.fori_loop(0, tb // 2, body, 0)
    insert(0, tb - 2)
    insert(1, tb - 1)
    w_ref[...] = gate_ref[...] * _gelu(hid_ref[...])


def _peer_v_kernel(idx_ref, w_ref, tab_ref, h_ref, g_ref, o_ref, tile_a, tile_b, y_ref, *, tb):
    lane = lax.broadcasted_iota(I32, (N_SEL, tb), 1)

    def weighted_sum(tile_ref, t):
        wcol = jnp.sum(jnp.where(lane == t, w_ref[...], 0.0), axis=1, keepdims=True)
        for s in range(ROW_SUBLANES):
            lo, hi = _unpack(tile_ref, s)
            y_ref[pl.ds(t, 1), s, :] = jnp.sum(lo * wcol, axis=0, keepdims=True)
            y_ref[pl.ds(t, 1), ROW_SUBLANES + s, :] = jnp.sum(hi * wcol, axis=0, keepdims=True)

    _gather_rows(idx_ref, tab_ref, tile_a, 0)

    def body(p, carry):
        t = 2 * p
        _gather_rows(idx_ref, tab_ref, tile_b, t + 1)
        weighted_sum(tile_a, t)
        _gather_rows(idx_ref, tab_ref, tile_a, _next_token(t + 2, tb))
        weighted_sum(tile_b, t + 1)
        return carry

    lax.fori_loop(0, tb // 2, body, 0)
    nchunk = 2 * ROW_SUBLANES
    hs = [h_ref[:, c * LANES:(c + 1) * LANES] + y_ref[:, c, :] for c in range(nchunk)]
    ms = sum(jnp.sum(hc * hc, axis=-1, keepdims=True) for hc in hs) * (1.0 / D_MODEL)
    scale = lax.rsqrt(ms + EPS)
    for c in range(nchunk):
        o_ref[:, c * LANES:(c + 1) * LANES] = hs[c] * scale * g_ref[:, c * LANES:(c + 1) * LANES]


def _peer_specs(tb):
    idx_spec = pl.BlockSpec((None, tb, N_SEL), lambda i: (i, 0, 0), memory_space=pltpu.SMEM)
    blk_spec = pl.BlockSpec((None, N_SEL, tb), lambda i: (i, 0, 0))
    tok_spec = pl.BlockSpec((tb, 2 * ROW_SUBLANES, LANES), lambda i: (i, 0, 0))
    tab_spec = pl.BlockSpec(memory_space=pltpu.VMEM)
    tile = pltpu.VMEM((ROW_SUBLANES * TILE_STRIDE, LANES), U32)
    return idx_spec, blk_spec, tok_spec, tab_spec, [tile, tile]


def _peer_u(idx, xn3, gate, tab, tb=LANES):
    nblk = idx.shape[0]
    idx_spec, blk_spec, tok_spec, tab_spec, tiles = _peer_specs(tb)
    return pl.pallas_call(
        functools.partial(_peer_u_kernel, tb=tb),
        grid=(nblk,),
        in_specs=[idx_spec, tok_spec, blk_spec, tab_spec],
        out_specs=blk_spec,
        out_shape=jax.ShapeDtypeStruct((nblk, N_SEL, tb), F32),
        scratch_shapes=tiles + [pltpu.VMEM((N_SEL, tb), F32), pltpu.VMEM((2, N_SEL, LANES), F32)],
        compiler_params=_params("arbitrary"),
        name="peer_u",
    )(idx, xn3, gate, tab)


def _peer_v(idx, w, tab, h, g, tb=LANES):
    nblk = idx.shape[0]
    idx_spec, blk_spec, _, tab_spec, tiles = _peer_specs(tb)
    row_spec = pl.BlockSpec((tb, D_MODEL), lambda i: (i, 0))
    return pl.pallas_call(
        functools.partial(_peer_v_kernel, tb=tb),
        grid=(nblk,),
        in_specs=[idx_spec, blk_spec, tab_spec, row_spec, pl.BlockSpec((1, D_MODEL), lambda i: (0, 0))],
        out_specs=row_spec,
        out_shape=jax.ShapeDtypeStruct((nblk * tb, D_MODEL), F32),
        scratch_shapes=tiles + [pltpu.VMEM((tb, 2 * ROW_SUBLANES, LANES), F32)],
        compiler_params=_params("arbitrary"),
        name="peer_v",
    )(idx, w, tab, h, g)


def _head_major_columns():
    h = jnp.arange(DA_HEADS)[:, None, None]
    m = jnp.arange(2)[None, :, None]
    d = jnp.arange(DA_DQK)[None, None, :]
    return (m * (DA_HEADS * DA_DQK) + h * DA_DQK + d).reshape(-1)


def kernel(x, norm1_g, w_in, lambda_q1, lambda_k1, lambda_q2, lambda_k2, da_subln_g, sg_ln_g, sg_ln_b,
           sg_w, sg_b, w_branch_attn, w_branch_sg, w_out, norm2_g, peer_w_query, peer_subkeys, peer_u,
           peer_v, final_g):
    b, s, d = x.shape
    t = b * s
    assert w_in.shape[0] == 1, "single-layer block"
    x2 = x.reshape(t, d)
    row = lambda a: a.reshape(1, -1)

    perm = _head_major_columns()
    qk = 2 * DA_HEADS * DA_DQK
    w = w_in[0]
    w = jnp.concatenate([w[:, :qk][:, perm], w[:, qk:2 * qk][:, perm], w[:, 2 * qk:]], axis=1).astype(BF16)

    proj = _inproj(x2, row(norm1_g[0]), w, row(sg_ln_g[0]), row(sg_ln_b[0]))
    ya = _attention(proj.reshape(b, s, -1), row(lambda_q1[0]), row(lambda_k1[0]), row(lambda_q2[0]),
                    row(lambda_k2[0]), da_subln_g[0].reshape(-1, 1))
    h, xn, qp = _mix(proj, ya.reshape(t, -1), x2, sg_w[0], sg_b[0][:, :, None],
                     w_branch_attn[0].astype(BF16), w_branch_sg[0].astype(BF16), w_out[0].astype(BF16),
                     row(norm2_g[0]), peer_w_query[0].astype(BF16))
    idx, gate = _topk(qp, peer_subkeys[0])
    wts = _peer_u(idx, xn.reshape(t, 2 * ROW_SUBLANES, LANES), gate, _pack_table(peer_u[0]))
    out = _peer_v(idx, wts, _pack_table(peer_v[0]), h, row(final_g))
    return out.reshape(b, s, d)
```
